```python
import math
import jax
import jax.numpy as jnp
from jax import lax
import numpy as np

D_MODEL = 1024
BATCH = 2
SEQ = 8192
DEPTH = 2

N_META = 16
EPS = 1e-6

HY_WIDTH = D_MODEL
HY_ORDER = 2
HY_SHORT = 3
HY_BANDS = 16
HY_EMB = 2 * HY_BANDS + 1
HY_FILTER_HIDDEN = 64
HY_DECAY_PCT_SHORT = 0.3
HY_DECAY_PCT_LONG = 1.5
HY_DECAY_TARGET = 1e-2
HY_FILTER_OUT_SCALE = 0.02

LRU_WIDTH = D_MODEL
LRU_BLOCKS = 8
LRU_BLOCK = LRU_WIDTH // LRU_BLOCKS
LRU_CONV = 4
LRU_C = 8.0
LRU_A_MIN = 0.9
LRU_A_MAX = 0.999

RET_HEADS = 4
RET_DK = D_MODEL // RET_HEADS
RET_DV = D_MODEL // RET_HEADS
RET_WIDTH = RET_HEADS * RET_DV
RET_CHUNK = 128
ROPE_BASE = 10000.0

N_BRANCH = 3
BRANCH_WIDTH = D_MODEL

D_FF = ((8 * D_MODEL + 3 * 256 - 1) // (3 * 256)) * 256

OFF_LRU_X = HY_ORDER * HY_WIDTH + HY_WIDTH
OFF_LRU_G = OFF_LRU_X + LRU_WIDTH
OFF_Q = OFF_LRU_G + LRU_WIDTH
OFF_K = OFF_Q + RET_HEADS * RET_DK
OFF_V = OFF_K + RET_HEADS * RET_DK
OFF_G = OFF_V + RET_WIDTH
OFF_MERGE = OFF_G + RET_WIDTH
D_IN = OFF_MERGE + N_BRANCH * D_MODEL

kernel_name = "bidir_hyena_rglru_retention_hybrid"


def rmsnorm(x, g):
    xf = x.astype(jnp.float32)
    y = xf * lax.rsqrt(jnp.mean(xf * xf, axis=-1, keepdims=True) + EPS)
    return (y * g.astype(jnp.float32)).astype(x.dtype)


def depthwise_conv(x, w, b, pad_lo, pad_hi):
    c = x.shape[-1]
    y = lax.conv_general_dilated(
        x, w[:, None, :].astype(x.dtype), window_strides=(1,),
        padding=[(pad_lo, pad_hi)], dimension_numbers=('NWC', 'WIO', 'NWC'),
        feature_group_count=c)
    return y + b.astype(x.dtype)


def hyena_filters(L, w1, b1, w2, b2, w3, b3, freq, wout):
    f32 = jnp.float32
    t = jnp.arange(L, dtype=f32)
    t_norm = t / max(L - 1, 1)
    bands = jnp.linspace(1e-4, HY_BANDS - 1, HY_BANDS, dtype=f32)
    ang = (2.0 * math.pi / L) * t[:, None] * bands[None, :]
    z = jnp.concatenate([t_norm[:, None], jnp.cos(ang), jnp.sin(ang)], axis=-1)
    fr = freq.astype(f32)
    hdn = jnp.sin(fr * (z @ w1.astype(f32) + b1.astype(f32)))
    hdn = jnp.sin(fr * (hdn @ w2.astype(f32) + b2.astype(f32)))
    hdn = jnp.sin(fr * (hdn @ w3.astype(f32) + b3.astype(f32)))
    k = (hdn @ wout.astype(f32)).reshape(L, HY_ORDER, 2, HY_WIDTH)
    max_decay = math.log(HY_DECAY_TARGET) / HY_DECAY_PCT_SHORT
    min_decay = math.log(HY_DECAY_TARGET) / HY_DECAY_PCT_LONG
    deltas = jnp.abs(jnp.linspace(min_decay, max_decay, HY_WIDTH, dtype=f32))
    window = jnp.exp(-t_norm[:, None] * deltas[None, :])
    k = k * window[:, None, None, :]
    kk = jnp.concatenate([k[:, :, 0], jnp.zeros((1, HY_ORDER, HY_WIDTH), f32), k[:0:-1, :, 1]], axis=0)
    return jnp.fft.rfft(kk, axis=0)


def hyena_mixer(proj, conv_w, conv_b, kf, skip):
    L = proj.shape[1]
    uc = depthwise_conv(proj, conv_w, conv_b, HY_SHORT // 2, HY_SHORT // 2)
    x1 = uc[..., :HY_WIDTH]
    x2 = uc[..., HY_WIDTH:2 * HY_WIDTH]
    v = uc[..., 2 * HY_WIDTH:]
    z = v.astype(jnp.float32)
    skf = skip.astype(jnp.float32)
    for o, gate in enumerate((x1, x2)):
        zf = jnp.fft.rfft(z, n=2 * L, axis=1)
        conv = jnp.fft.irfft(zf * kf[None, :, o, :], n=2 * L, axis=1)[:, :L]
        z = gate.astype(jnp.float32) * (conv + z * skf[o])
    return z.astype(proj.dtype)


def linear_scan(a, b, reverse):
    def combine(left, right):
        a_l, b_l = left
        a_r, b_r = right
        return a_l * a_r, a_r * b_l + b_r
    _, h = lax.associative_scan(combine, (a, b), reverse=reverse, axis=1)
    return h


def rglru_mixer(xb, gb, conv_w, conv_b, gate_w, gate_b, lam):
    f32 = jnp.float32
    x = depthwise_conv(xb, conv_w, conv_b, LRU_CONV // 2, LRU_CONV - 1 - LRU_CONV // 2)
    B, T, _ = x.shape
    xblk = x.reshape(B, T, LRU_BLOCKS, LRU_BLOCK)
    gates = jnp.einsum('btnc,dgncm->dgbtnm', xblk, gate_w.astype(x.dtype)).reshape(2, 2, B, T, LRU_WIDTH)
    gates = jax.nn.sigmoid(gates.astype(f32) + gate_b.astype(f32)[:, :, None, None, :])
    r, i = gates[:, 0], gates[:, 1]
    log_a = -LRU_C * r * jax.nn.softplus(-lam.astype(f32))[:, None, None, :]
    a = jnp.exp(log_a)
    b = jnp.sqrt(-jnp.expm1(2.0 * log_a)) * i * x.astype(f32)[None]
    h_f = linear_scan(a[0], b[0], reverse=False)
    h_b = linear_scan(a[1], b[1], reverse=True)
    y = (h_f + h_b) * jax.nn.gelu(gb.astype(f32))
    return y.astype(xb.dtype)


def rotary(x, pos):
    half = x.shape[-1] // 2
    inv = ROPE_BASE ** (-jnp.linspace(0.0, 1.0, half, dtype=jnp.float32))
    ang = pos[:, None] * inv[None, :]
    cos = jnp.cos(ang)[None, :, None, :]
    sin = jnp.sin(ang)[None, :, None, :]
    x1, x2 = x[..., :half], x[..., half:]
    return jnp.concatenate([x1 * cos - x2 * sin, x1 * sin + x2 * cos], axis=-1)


def retention_chunks(q, k, v, log_gamma, strict):
    C = q.shape[2]
    idx = jnp.arange(C, dtype=jnp.float32)
    diff = idx[:, None] - idx[None, :]
    mask = (diff > 0) if strict else (diff >= 0)
    decay_in = jnp.where(mask[None], jnp.exp(jnp.where(mask, diff, 0.0)[None] * log_gamma[:, None, None]), 0.0)
    scores = jnp.einsum('bnchd,bnshd->bnhcs', q, k) * decay_in
    inner = jnp.einsum('bnhcs,bnshe->bnche', scores, v)
    zeta = jnp.exp((C - 1 - idx)[:, None] * log_gamma[None, :])
    kv = jnp.einsum('bnshd,sh,bnshe->nbhde', k, zeta, v)
    chunk_decay = jnp.exp(C * log_gamma)[None, :, None, None]

    def step(state, kv_n):
        return chunk_decay * state + kv_n, state

    _, prev = lax.scan(step, jnp.zeros_like(kv[0]), kv)
    xi = jnp.exp((idx + 1.0)[:, None] * log_gamma[None, :])
    cross = jnp.einsum('bnchd,ch,nbhde->bnche', q, xi, prev)
    return inner + cross


def retention_mixer(q, k, v, g):
    f32 = jnp.float32
    B, T, _ = q.shape
    pos = jnp.arange(T, dtype=f32)
    qh = rotary(q.astype(f32).reshape(B, T, RET_HEADS, RET_DK), pos)
    kh = rotary(k.astype(f32).reshape(B, T, RET_HEADS, RET_DK), pos) * (RET_DK ** -0.5)
    vh = v.astype(f32).reshape(B, T, RET_HEADS, RET_DV)
    pad = RET_CHUNK - N_META
    P = T + pad
    n_chunks = P // RET_CHUNK
    padf = lambda t: jnp.pad(t, ((0, 0), (pad, 0), (0, 0), (0, 0)))
    chunk = lambda t: t.reshape(B, n_chunks, RET_CHUNK, RET_HEADS, t.shape[-1])
    flip = lambda t: jnp.flip(t, axis=1)
    qp, kp, vp = padf(qh), padf(kh), padf(vh)
    log_gamma = jnp.log1p(-(2.0 ** (-5.0 - jnp.arange(RET_HEADS, dtype=f32))))
    y_f = retention_chunks(chunk(qp), chunk(kp), chunk(vp), log_gamma, strict=False)
    y_b = retention_chunks(chunk(flip(qp)), chunk(flip(kp)), chunk(flip(vp)), log_gamma, strict=True)
    y = y_f.reshape(B, P, RET_HEADS, RET_DV) + flip(y_b.reshape(B, P, RET_HEADS, RET_DV))
    y = y[:, pad:]
    mu = jnp.mean(y, axis=-1, keepdims=True)
    var = jnp.mean(jnp.square(y - mu), axis=-1, keepdims=True)
    yn = ((y - mu) * lax.rsqrt(var + EPS)).reshape(B, T, RET_WIDTH)
    return (jax.nn.silu(g.astype(f32)) * yn).astype(q.dtype)


def hybrid_layer(h, norm1_g, w_in, hy_conv_w, hy_conv_b, hy_kf, hy_skip,
                 lru_conv_w, lru_conv_b, lru_gate_w, lru_gate_b, lru_lambda,
                 w_branch, w_out, norm2_g, ffn_w_gu, ffn_w_down):
    B, T, _ = h.shape
    u = rmsnorm(h, norm1_g)
    p = jnp.einsum('btd,de->bte', u, w_in)
    y_hy = hyena_mixer(p[..., :OFF_LRU_X], hy_conv_w, hy_conv_b, hy_kf, hy_skip)
    y_lru = rglru_mixer(p[..., OFF_LRU_X:OFF_LRU_G], p[..., OFF_LRU_G:OFF_Q],
                        lru_conv_w, lru_conv_b, lru_gate_w, lru_gate_b, lru_lambda)
    y_ret = retention_mixer(p[..., OFF_Q:OFF_K], p[..., OFF_K:OFF_V],
                            p[..., OFF_V:OFF_G], p[..., OFF_G:OFF_MERGE])
    gates = jax.nn.sigmoid(p[..., OFF_MERGE:].astype(jnp.float32)).reshape(B, T, N_BRANCH, D_MODEL)
    merged = jnp.zeros((B, T, D_MODEL), jnp.float32)
    for bi, yb in enumerate((y_hy, y_lru, y_ret)):
        proj = jnp.einsum('btc,cd->btd', yb, w_branch[bi]).astype(jnp.float32)
        merged = merged + gates[:, :, bi] * proj
    h = h + jnp.einsum('btd,de->bte', merged.astype(h.dtype), w_out)
    v = rmsnorm(h, norm2_g)
    gu = jnp.einsum('btd,df->btf', v, ffn_w_gu)
    hid = jax.nn.silu(gu[..., :D_FF]) * gu[..., D_FF:]
    h = h + jnp.einsum('btf,fd->btd', hid, ffn_w_down)
    return h


def setup_inputs(seed: int = 0) -> dict:
    key = jax.random.key(seed)
    ks = jax.random.split(key, 32)
    f32 = jnp.float32
    H = HY_FILTER_HIDDEN

    def nrm(k, shape, scale):
        return jax.random.normal(k, shape, f32) * scale

    u = jax.random.uniform(ks[19], (DEPTH, 2, LRU_WIDTH), f32, LRU_A_MIN, LRU_A_MAX)
    s = u ** (1.0 / LRU_C)
    return {
        "x": nrm(ks[0], (BATCH, SEQ, D_MODEL), 1.0),
        "meta_tokens": nrm(ks[1], (N_META, D_MODEL), 1.0),
        "norm1_g": 1.0 + nrm(ks[2], (DEPTH, D_MODEL), 0.02),
        "w_in": nrm(ks[3], (DEPTH, D_MODEL, D_IN), D_MODEL ** -0.5),
        "hy_conv_w": nrm(ks[4], (DEPTH, HY_SHORT, (HY_ORDER + 1) * HY_WIDTH), HY_SHORT ** -0.5),
        "hy_conv_b": nrm(ks[5], (DEPTH, (HY_ORDER + 1) * HY_WIDTH), 0.02),
        "hy_f_w1": nrm(ks[6], (DEPTH, HY_EMB, H), HY_EMB ** -0.5),
        "hy_f_b1": nrm(ks[7], (DEPTH, H), 0.02),
        "hy_f_w2": nrm(ks[8], (DEPTH, H, H), H ** -0.5),
        "hy_f_b2": nrm(ks[9], (DEPTH, H), 0.02),
        "hy_f_w3": nrm(ks[10], (DEPTH, H, H), H ** -0.5),
        "hy_f_b3": nrm(ks[11], (DEPTH, H), 0.02),
        "hy_f_freq": 1.0 + nrm(ks[12], (DEPTH, H), 0.02),
        "hy_f_wout": nrm(ks[13], (DEPTH, H, HY_ORDER * 2 * HY_WIDTH), HY_FILTER_OUT_SCALE * H ** -0.5),
        "hy_skip": nrm(ks[14], (DEPTH, HY_ORDER, HY_WIDTH), 1.0),
        "lru_conv_w": nrm(ks[15], (DEPTH, LRU_CONV, LRU_WIDTH), LRU_CONV ** -0.5),
        "lru_conv_b": nrm(ks[16], (DEPTH, LRU_WIDTH), 0.02),
        "lru_gate_w": nrm(ks[17], (DEPTH, 2, 2, LRU_BLOCKS, LRU_BLOCK, LRU_BLOCK), LRU_BLOCK ** -0.5),
        "lru_gate_b": nrm(ks[18], (DEPTH, 2, 2, LRU_WIDTH), 0.02),
        "lru_lambda": jnp.log(s) - jnp.log1p(-s),
        "w_branch": nrm(ks[20], (DEPTH, N_BRANCH, BRANCH_WIDTH, D_MODEL), BRANCH_WIDTH ** -0.5),
        "w_out": nrm(ks[21], (DEPTH, D_MODEL, D_MODEL), D_MODEL ** -0.5),
        "norm2_g": 1.0 + nrm(ks[22], (DEPTH, D_MODEL), 0.02),
        "ffn_w_gu": nrm(ks[23], (DEPTH, D_MODEL, 2 * D_FF), D_MODEL ** -0.5),
        "ffn_w_down": nrm(ks[24], (DEPTH, D_FF, D_MODEL), D_FF ** -0.5),
        "final_g": 1.0 + nrm(ks[25], (D_MODEL,), 0.02),
    }


def reference(x, meta_tokens, norm1_g, w_in, hy_conv_w, hy_conv_b, hy_f_w1, hy_f_b1,
              hy_f_w2, hy_f_b2, hy_f_w3, hy_f_b3, hy_f_freq, hy_f_wout, hy_skip,
              lru_conv_w, lru_conv_b, lru_gate_w, lru_gate_b, lru_lambda,
              w_branch, w_out, norm2_g, ffn_w_gu, ffn_w_down, final_g):
    B = x.shape[0]
    meta = jnp.broadcast_to(meta_tokens[None].astype(x.dtype), (B, N_META, D_MODEL))
    h = jnp.concatenate([meta, x], axis=1)
    T = h.shape[1]
    for l in range(DEPTH):
        hy_kf = hyena_filters(T, hy_f_w1[l], hy_f_b1[l], hy_f_w2[l], hy_f_b2[l],
                              hy_f_w3[l], hy_f_b3[l], hy_f_freq[l], hy_f_wout[l])
        h = hybrid_layer(h, norm1_g[l], w_in[l], hy_conv_w[l], hy_conv_b[l], hy_kf, hy_skip[l],
                         lru_conv_w[l], lru_conv_b[l], lru_gate_w[l], lru_gate_b[l], lru_lambda[l],
                         w_branch[l], w_out[l], norm2_g[l], ffn_w_gu[l], ffn_w_down[l])
    out = rmsnorm(h, final_g)
    return out[:, N_META:]
```

```python
import functools
import math

import jax
import jax.numpy as jnp
import numpy as np
from jax import lax
from jax.experimental import pallas as pl
from jax.experimental.pallas import tpu as pltpu

F32 = jnp.float32
BF16 = jnp.bfloat16

D_MODEL = 1024
N_META = 16
EPS = 1e-6
DEPTH = 2

HY_WIDTH = D_MODEL
HY_ORDER = 2
HY_SHORT = 3
HY_BANDS = 16
HY_DECAY_PCT_SHORT = 0.3
HY_DECAY_PCT_LONG = 1.5
HY_DECAY_TARGET = 1e-2

LRU_WIDTH = D_MODEL
LRU_BLOCKS = 8
LRU_BLOCK = LRU_WIDTH // LRU_BLOCKS
LRU_CONV = 4
LRU_C = 8.0

RET_HEADS = 4
RET_DK = D_MODEL // RET_HEADS
RET_DV = D_MODEL // RET_HEADS
RET_WIDTH = RET_HEADS * RET_DV
RET_CHUNK = 128
ROPE_BASE = 10000.0

N_BRANCH = 3
D_FF = ((8 * D_MODEL + 3 * 256 - 1) // (3 * 256)) * 256

OFF_LRU_X = HY_ORDER * HY_WIDTH + HY_WIDTH
OFF_LRU_G = OFF_LRU_X + LRU_WIDTH
OFF_Q = OFF_LRU_G + LRU_WIDTH
OFF_K = OFF_Q + RET_HEADS * RET_DK
OFF_V = OFF_K + RET_HEADS * RET_DK
OFF_G = OFF_V + RET_WIDTH
OFF_MERGE = OFF_G + RET_WIDTH
D_IN = OFF_MERGE + N_BRANCH * D_MODEL

ROW_TILE = 912
MERGE_ROW_TILE = 432
VMEM_LIMIT = 56 * 1024 * 1024


def _rms(x, g):
    ms = jnp.mean(x * x, axis=-1, keepdims=True)
    return x * lax.rsqrt(ms + EPS) * g


def _norm_proj_kernel(h_ref, g_ref, w_ref, o_ref, u_scr):
    @pl.when(pl.program_id(1) == 0)
    def _():
        u_scr[...] = _rms(h_ref[...], g_ref[...]).astype(BF16)

    o_ref[...] = jnp.dot(u_scr[...], w_ref[...], preferred_element_type=F32)


def norm_proj(h2, g, w_bf16, tn=1024):
    m, d = h2.shape
    n = w_bf16.shape[1]
    return pl.pallas_call(
        _norm_proj_kernel,
        grid=(m // ROW_TILE, n // tn),
        in_specs=[
            pl.BlockSpec((ROW_TILE, d), lambda i, j: (i, 0)),
            pl.BlockSpec((1, d), lambda i, j: (0, 0)),
            pl.BlockSpec((d, tn), lambda i, j: (0, j)),
        ],
        out_specs=pl.BlockSpec((ROW_TILE, tn), lambda i, j: (i, j)),
        out_shape=jax.ShapeDtypeStruct((m, n), F32),
        scratch_shapes=[pltpu.VMEM((ROW_TILE, d), BF16)],
        compiler_params=pltpu.CompilerParams(
            dimension_semantics=("parallel", "arbitrary"), vmem_limit_bytes=VMEM_LIMIT),
        name="norm_proj",
    )(h2, g.reshape(1, d), w_bf16)


def _merge_kernel(h_ref, yh_ref, yl_ref, yr_ref, gate_ref, wb_ref, wo_ref, o_ref):
    merged = None
    for bi, y_ref in enumerate((yh_ref, yl_ref, yr_ref)):
        proj = jnp.dot(y_ref[...].astype(BF16), wb_ref[bi], preferred_element_type=F32)
        term = jax.nn.sigmoid(gate_ref[:, bi * D_MODEL:(bi + 1) * D_MODEL]) * proj
        merged = term if merged is None else merged + term
    o_ref[...] = h_ref[...] + jnp.dot(merged.astype(BF16), wo_ref[...], preferred_element_type=F32)


def merge_out(h2, y_hy, y_lru, y_ret, p, wb_bf16, wo_bf16):
    m, d = h2.shape
    row = lambda i: (i, 0)
    gate_blk = OFF_MERGE // (N_BRANCH * D_MODEL)
    tm = MERGE_ROW_TILE
    return pl.pallas_call(
        _merge_kernel,
        grid=(m // tm,),
        in_specs=[
            pl.BlockSpec((tm, d), row),
            pl.BlockSpec((tm, d), row),
            pl.BlockSpec((tm, d), row),
            pl.BlockSpec((tm, d), row),
            pl.BlockSpec((tm, N_BRANCH * D_MODEL), lambda i: (i, gate_blk)),
            pl.BlockSpec((N_BRANCH, d, d), lambda i: (0, 0, 0), pipeline_mode=pl.Buffered(1)),
            pl.BlockSpec((d, d), lambda i: (0, 0), pipeline_mode=pl.Buffered(1)),
        ],
        out_specs=pl.BlockSpec((tm, d), row),
        out_shape=jax.ShapeDtypeStruct((m, d), F32),
        compiler_params=pltpu.CompilerParams(
            dimension_semantics=("parallel",), vmem_limit_bytes=VMEM_LIMIT),
        name="merge_out",
    )(h2, y_hy, y_lru, y_ret, p, wb_bf16, wo_bf16)


FF_CHUNK = 256


def _ffn_kernel(h_ref, g_ref, wgu_ref, wd_ref, fg_ref, o_ref, v_scr, acc_scr, *, final_norm):
    h = h_ref[...]
    v_scr[...] = _rms(h, g_ref[...]).astype(BF16)
    acc_scr[...] = h

    def body(c, carry):
        off = pl.multiple_of(c * FF_CHUNK, FF_CHUNK)
        v = v_scr[...]
        gate = jnp.dot(v, wgu_ref[:, pl.ds(off, FF_CHUNK)], preferred_element_type=F32)
        up = jnp.dot(v, wgu_ref[:, pl.ds(D_FF + off, FF_CHUNK)], preferred_element_type=F32)
        hid = (gate * jax.nn.sigmoid(gate) * up).astype(BF16)
        acc_scr[...] += jnp.dot(hid, wd_ref[pl.ds(off, FF_CHUNK), :], preferred_element_type=F32)
        return carry

    lax.fori_loop(0, D_FF // FF_CHUNK, body, 0)
    out = acc_scr[...]
    if final_norm:
        out = _rms(out, fg_ref[...])
    o_ref[...] = out


def ffn(h2, g, wgu_bf16, wd_bf16, final_g, final_norm):
    m, d = h2.shape
    row = lambda i: (i, 0)
    return pl.pallas_call(
        functools.partial(_ffn_kernel, final_norm=final_norm),
        grid=(m // ROW_TILE,),
        in_specs=[
            pl.BlockSpec((ROW_TILE, d), row),
            pl.BlockSpec((1, d), lambda i: (0, 0)),
            pl.BlockSpec((d, 2 * D_FF), lambda i: (0, 0), pipeline_mode=pl.Buffered(1)),
            pl.BlockSpec((D_FF, d), lambda i: (0, 0), pipeline_mode=pl.Buffered(1)),
            pl.BlockSpec((1, d), lambda i: (0, 0)),
        ],
        out_specs=pl.BlockSpec((ROW_TILE, d), row),
        out_shape=jax.ShapeDtypeStruct((m, d), F32),
        scratch_shapes=[pltpu.VMEM((ROW_TILE, d), BF16), pltpu.VMEM((ROW_TILE, d), F32)],
        compiler_params=pltpu.CompilerParams(
            dimension_semantics=("parallel",), vmem_limit_bytes=VMEM_LIMIT),
        name="ffn",
    )(h2, g.reshape(1, d), wgu_bf16, wd_bf16, final_g.reshape(1, d))


def _depthwise_conv(x, w, b, pad_lo, pad_hi):
    c = x.shape[-1]
    y = lax.conv_general_dilated(
        x, w[:, None, :].astype(x.dtype), window_strides=(1,),
        padding=[(pad_lo, pad_hi)], dimension_numbers=('NWC', 'WIO', 'NWC'),
        feature_group_count=c)
    return y + b.astype(x.dtype)


def _hyena_filters(L, w1, b1, w2, b2, w3, b3, freq, wout):
    t = jnp.arange(L, dtype=F32)
    t_norm = t / max(L - 1, 1)
    bands = jnp.linspace(1e-4, HY_BANDS - 1, HY_BANDS, dtype=F32)
    ang = (2.0 * math.pi / L) * t[:, None] * bands[None, :]
    z = jnp.concatenate([t_norm[:, None], jnp.cos(ang), jnp.sin(ang)], axis=-1)
    hdn = jnp.sin(freq * (z @ w1 + b1))
    hdn = jnp.sin(freq * (hdn @ w2 + b2))
    hdn = jnp.sin(freq * (hdn @ w3 + b3))
    k = (hdn @ wout).reshape(L, HY_ORDER, 2, HY_WIDTH)
    max_decay = math.log(HY_DECAY_TARGET) / HY_DECAY_PCT_SHORT
    min_decay = math.log(HY_DECAY_TARGET) / HY_DECAY_PCT_LONG
    deltas = jnp.abs(jnp.linspace(min_decay, max_decay, HY_WIDTH, dtype=F32))
    window = jnp.exp(-t_norm[:, None] * deltas[None, :])
    k = k * window[:, None, None, :]
    kk = jnp.concatenate([k[:, :, 0], jnp.zeros((1, HY_ORDER, HY_WIDTH), F32), k[:0:-1, :, 1]], axis=0)
    return jnp.fft.rfft(kk, axis=0)


def _hyena_mixer(proj, conv_w, conv_b, kf, skip):
    L = proj.shape[1]
    uc = _depthwise_conv(proj, conv_w, conv_b, HY_SHORT // 2, HY_SHORT // 2)
    x1 = uc[..., :HY_WIDTH]
    x2 = uc[..., HY_WIDTH:2 * HY_WIDTH]
    z = uc[..., 2 * HY_WIDTH:]
    for o, gate in enumerate((x1, x2)):
        zf = jnp.fft.rfft(z, n=2 * L, axis=1)
        conv = jnp.fft.irfft(zf * kf[None, :, o, :], n=2 * L, axis=1)[:, :L]
        z = gate * (conv + z * skip[o])
    return z


def _linear_scan(a, b, reverse):
    def combine(left, right):
        a_l, b_l = left
        a_r, b_r = right
        return a_l * a_r, a_r * b_l + b_r
    _, h = lax.associative_scan(combine, (a, b), reverse=reverse, axis=1)
    return h


def _rglru_mixer(xb, gb, conv_w, conv_b, gate_w, gate_b, lam):
    x = _depthwise_conv(xb, conv_w, conv_b, LRU_CONV // 2, LRU_CONV - 1 - LRU_CONV // 2)
    B, T, _ = x.shape
    xblk = x.reshape(B, T, LRU_BLOCKS, LRU_BLOCK)
    gates = jnp.einsum('btnc,dgncm->dgbtnm', xblk, gate_w).reshape(2, 2, B, T, LRU_WIDTH)
    gates = jax.nn.sigmoid(gates + gate_b[:, :, None, None, :])
    r, i = gates[:, 0], gates[:, 1]
    log_a = -LRU_C * r * jax.nn.softplus(-lam)[:, None, None, :]
    a = jnp.exp(log_a)
    b = jnp.sqrt(-jnp.expm1(2.0 * log_a)) * i * x[None]
    h_f = _linear_scan(a[0], b[0], reverse=False)
    h_b = _linear_scan(a[1], b[1], reverse=True)
    return (h_f + h_b) * jax.nn.gelu(gb)


def _rotary(x, pos):
    half = x.shape[-1] // 2
    inv = ROPE_BASE ** (-jnp.linspace(0.0, 1.0, half, dtype=F32))
    ang = pos[:, None] * inv[None, :]
    cos = jnp.cos(ang)[None, :, None, :]
    sin = jnp.sin(ang)[None, :, None, :]
    x1, x2 = x[..., :half], x[..., half:]
    return jnp.concatenate([x1 * cos - x2 * sin, x1 * sin + x2 * cos], axis=-1)


def _retention_chunks(q, k, v, log_gamma, strict):
    C = q.shape[2]
    idx = jnp.arange(C, dtype=F32)
    diff = idx[:, None] - idx[None, :]
    mask = (diff > 0) if strict else (diff >= 0)
    decay_in = jnp.where(mask[None], jnp.exp(jnp.where(mask, diff, 0.0)[None] * log_gamma[:, None, None]), 0.0)
    scores = jnp.einsum('bnchd,bnshd->bnhcs', q, k) * decay_in
    inner = jnp.einsum('bnhcs,bnshe->bnche', scores, v)
    zeta = jnp.exp((C - 1 - idx)[:, None] * log_gamma[None, :])
    kv = jnp.einsum('bnshd,sh,bnshe->nbhde', k, zeta, v)
    chunk_decay = jnp.exp(C * log_gamma)[None, :, None, None]

    def step(state, kv_n):
        return chunk_decay * state + kv_n, state

    _, prev = lax.scan(step, jnp.zeros_like(kv[0]), kv)
    xi = jnp.exp((idx + 1.0)[:, None] * log_gamma[None, :])
    cross = jnp.einsum('bnchd,ch,nbhde->bnche', q, xi, prev)
    return inner + cross


def _retention_mixer(q, k, v, g):
    B, T, _ = q.shape
    pos = jnp.arange(T, dtype=F32)
    qh = _rotary(q.reshape(B, T, RET_HEADS, RET_DK), pos)
    kh = _rotary(k.reshape(B, T, RET_HEADS, RET_DK), pos) * (RET_DK ** -0.5)
    vh = v.reshape(B, T, RET_HEADS, RET_DV)
    pad = RET_CHUNK - N_META
    P = T + pad
    n_chunks = P // RET_CHUNK
    padf = lambda t: jnp.pad(t, ((0, 0), (pad, 0), (0, 0), (0, 0)))
    chunk = lambda t: t.reshape(B, n_chunks, RET_CHUNK, RET_HEADS, t.shape[-1])
    flip = lambda t: jnp.flip(t, axis=1)
    qp, kp, vp = padf(qh), padf(kh), padf(vh)
    log_gamma = jnp.log1p(-(2.0 ** (-5.0 - jnp.arange(RET_HEADS, dtype=F32))))
    y_f = _retention_chunks(chunk(qp), chunk(kp), chunk(vp), log_gamma, strict=False)
    y_b = _retention_chunks(chunk(flip(qp)), chunk(flip(kp)), chunk(flip(vp)), log_gamma, strict=True)
    y = y_f.reshape(B, P, RET_HEADS, RET_DV) + flip(y_b.reshape(B, P, RET_HEADS, RET_DV))
    y = y[:, pad:]
    mu = jnp.mean(y, axis=-1, keepdims=True)
    var = jnp.mean(jnp.square(y - mu), axis=-1, keepdims=True)
    yn = ((y - mu) * lax.rsqrt(var + EPS)).reshape(B, T, RET_WIDTH)
    return jax.nn.silu(g) * yn


def kernel(x, meta_tokens, norm1_g, w_in, hy_conv_w, hy_conv_b, hy_f_w1, hy_f_b1, hy_f_w2, hy_f_b2, hy_f_w3, hy_f_b3, hy_f_freq, hy_f_wout, hy_skip, lru_conv_w, lru_conv_b, lru_gate_w, lru_gate_b, lru_lambda, w_branch, w_out, norm2_g, ffn_w_gu, ffn_w_down, final_g):
    B = x.shape[0]
    meta = jnp.broadcast_to(meta_tokens[None].astype(x.dtype), (B, N_META, D_MODEL))
    h = jnp.concatenate([meta, x], axis=1)
    T = h.shape[1]
    h2 = h.reshape(B * T, D_MODEL)
    for l in range(DEPTH):
        p2 = norm_proj(h2, norm1_g[l], w_in[l].astype(BF16))
        p = p2.reshape(B, T, D_IN)
        kf = _hyena_filters(T, hy_f_w1[l], hy_f_b1[l], hy_f_w2[l], hy_f_b2[l],
                            hy_f_w3[l], hy_f_b3[l], hy_f_freq[l], hy_f_wout[l])
        y_hy = _hyena_mixer(p[..., :OFF_LRU_X], hy_conv_w[l], hy_conv_b[l], kf, hy_skip[l])
        y_lru = _rglru_mixer(p[..., OFF_LRU_X:OFF_LRU_G], p[..., OFF_LRU_G:OFF_Q],
                             lru_conv_w[l], lru_conv_b[l], lru_gate_w[l], lru_gate_b[l], lru_lambda[l])
        y_ret = _retention_mixer(p[..., OFF_Q:OFF_K], p[..., OFF_K:OFF_V],
                                 p[..., OFF_V:OFF_G], p[..., OFF_G:OFF_MERGE])
        h2 = merge_out(h2, y_hy.reshape(B * T, -1), y_lru.reshape(B * T, -1), y_ret.reshape(B * T, -1),
                       p2, w_branch[l].astype(BF16), w_out[l].astype(BF16))
        h2 = ffn(h2, norm2_g[l], ffn_w_gu[l].astype(BF16), ffn_w_down[l].astype(BF16),
                 final_g, final_norm=(l == DEPTH - 1))
    return h2.reshape(B, T, D_MODEL)[:, N_META:]
```

```python
import functools
import math

import jax
import jax.numpy as jnp
import numpy as np
from jax import lax
from jax.experimental import pallas as pl
from jax.experimental.pallas import tpu as pltpu

F32 = jnp.float32
BF16 = jnp.bfloat16

D_MODEL = 1024
N_META = 16
EPS = 1e-6
DEPTH = 2

HY_WIDTH = D_MODEL
HY_ORDER = 2
HY_SHORT = 3
HY_BANDS = 16
HY_DECAY_PCT_SHORT = 0.3
HY_DECAY_PCT_LONG = 1.5
HY_DECAY_TARGET = 1e-2

LRU_WIDTH = D_MODEL
LRU_BLOCKS = 8
LRU_BLOCK = LRU_WIDTH // LRU_BLOCKS
LRU_CONV = 4
LRU_C = 8.0

RET_HEADS = 4
RET_DK = D_MODEL // RET_HEADS
RET_DV = D_MODEL // RET_HEADS
RET_WIDTH = RET_HEADS * RET_DV
RET_CHUNK = 128
ROPE_BASE = 10000.0

N_BRANCH = 3
D_FF = ((8 * D_MODEL + 3 * 256 - 1) // (3 * 256)) * 256

OFF_LRU_X = HY_ORDER * HY_WIDTH + HY_WIDTH
OFF_LRU_G = OFF_LRU_X + LRU_WIDTH
OFF_Q = OFF_LRU_G + LRU_WIDTH
OFF_K = OFF_Q + RET_HEADS * RET_DK
OFF_V = OFF_K + RET_HEADS * RET_DK
OFF_G = OFF_V + RET_WIDTH
OFF_MERGE = OFF_G + RET_WIDTH
D_IN = OFF_MERGE + N_BRANCH * D_MODEL

ROW_TILE = 912
MERGE_ROW_TILE = 432
VMEM_LIMIT = 56 * 1024 * 1024


def _rms(x, g):
    ms = jnp.mean(x * x, axis=-1, keepdims=True)
    return x * lax.rsqrt(ms + EPS) * g


def _norm_proj_kernel(h_ref, g_ref, w_ref, o_ref, u_scr):
    @pl.when(pl.program_id(1) == 0)
    def _():
        u_scr[...] = _rms(h_ref[...], g_ref[...]).astype(BF16)

    o_ref[...] = jnp.dot(u_scr[...], w_ref[...], preferred_element_type=F32)


def norm_proj(h2, g, w_bf16, tn=1024):
    m, d = h2.shape
    n = w_bf16.shape[1]
    return pl.pallas_call(
        _norm_proj_kernel,
        grid=(m // ROW_TILE, n // tn),
        in_specs=[
            pl.BlockSpec((ROW_TILE, d), lambda i, j: (i, 0)),
            pl.BlockSpec((1, d), lambda i, j: (0, 0)),
            pl.BlockSpec((d, tn), lambda i, j: (0, j)),
        ],
        out_specs=pl.BlockSpec((ROW_TILE, tn), lambda i, j: (i, j)),
        out_shape=jax.ShapeDtypeStruct((m, n), F32),
        scratch_shapes=[pltpu.VMEM((ROW_TILE, d), BF16)],
        compiler_params=pltpu.CompilerParams(
            dimension_semantics=("parallel", "arbitrary"), vmem_limit_bytes=VMEM_LIMIT),
        name="norm_proj",
    )(h2, g.reshape(1, d), w_bf16)


def _merge_kernel(h_ref, yh_ref, yl_ref, yr_ref, gate_ref, wb_ref, wo_ref, o_ref):
    merged = None
    for bi, y_ref in enumerate((yh_ref, yl_ref, yr_ref)):
        proj = jnp.dot(y_ref[...].astype(BF16), wb_ref[bi], preferred_element_type=F32)
        term = jax.nn.sigmoid(gate_ref[:, bi * D_MODEL:(bi + 1) * D_MODEL]) * proj
        merged = term if merged is None else merged + term
    o_ref[...] = h_ref[...] + jnp.dot(merged.astype(BF16), wo_ref[...], preferred_element_type=F32)


def merge_out(h, y_hy, y_lru, y_ret, p, wb_bf16, wo_bf16):
    B, T, d = h.shape
    tm = MERGE_ROW_TILE
    assert T % tm == 0
    row = lambda b, i: (b, i, 0)
    gate_blk = OFF_MERGE // (N_BRANCH * D_MODEL)
    blk = pl.BlockSpec((None, tm, d), row)
    return pl.pallas_call(
        _merge_kernel,
        grid=(B, T // tm),
        in_specs=[
            blk, blk, blk, blk,
            pl.BlockSpec((None, tm, N_BRANCH * D_MODEL), lambda b, i: (b, i, gate_blk)),
            pl.BlockSpec((N_BRANCH, d, d), lambda b, i: (0, 0, 0), pipeline_mode=pl.Buffered(1)),
            pl.BlockSpec((d, d), lambda b, i: (0, 0), pipeline_mode=pl.Buffered(1)),
        ],
        out_specs=blk,
        out_shape=jax.ShapeDtypeStruct((B, T, d), F32),
        compiler_params=pltpu.CompilerParams(
            dimension_semantics=("parallel", "parallel"), vmem_limit_bytes=VMEM_LIMIT),
        name="merge_out",
    )(h, y_hy, y_lru, y_ret, p, wb_bf16, wo_bf16)


FF_CHUNK = 256


def _ffn_kernel(h_ref, g_ref, wgu_ref, wd_ref, fg_ref, o_ref, v_scr, acc_scr, *, final_norm):
    h = h_ref[...]
    v_scr[...] = _rms(h, g_ref[...]).astype(BF16)
    acc_scr[...] = h

    def body(c, carry):
        off = pl.multiple_of(c * FF_CHUNK, FF_CHUNK)
        v = v_scr[...]
        gate = jnp.dot(v, wgu_ref[:, pl.ds(off, FF_CHUNK)], preferred_element_type=F32)
        up = jnp.dot(v, wgu_ref[:, pl.ds(D_FF + off, FF_CHUNK)], preferred_element_type=F32)
        hid = (gate * jax.nn.sigmoid(gate) * up).astype(BF16)
        acc_scr[...] += jnp.dot(hid, wd_ref[pl.ds(off, FF_CHUNK), :], preferred_element_type=F32)
        return carry

    lax.fori_loop(0, D_FF // FF_CHUNK, body, 0)
    out = acc_scr[...]
    if final_norm:
        out = _rms(out, fg_ref[...])
    o_ref[...] = out


def ffn(h2, g, wgu_bf16, wd_bf16, final_g, final_norm):
    m, d = h2.shape
    row = lambda i: (i, 0)
    return pl.pallas_call(
        functools.partial(_ffn_kernel, final_norm=final_norm),
        grid=(m // ROW_TILE,),
        in_specs=[
            pl.BlockSpec((ROW_TILE, d), row),
            pl.BlockSpec((1, d), lambda i: (0, 0)),
            pl.BlockSpec((d, 2 * D_FF), lambda i: (0, 0), pipeline_mode=pl.Buffered(1)),
            pl.BlockSpec((D_FF, d), lambda i: (0, 0), pipeline_mode=pl.Buffered(1)),
            pl.BlockSpec((1, d), lambda i: (0, 0)),
        ],
        out_specs=pl.BlockSpec((ROW_TILE, d), row),
        out_shape=jax.ShapeDtypeStruct((m, d), F32),
        scratch_shapes=[pltpu.VMEM((ROW_TILE, d), BF16), pltpu.VMEM((ROW_TILE, d), F32)],
        compiler_params=pltpu.CompilerParams(
            dimension_semantics=("parallel",), vmem_limit_bytes=VMEM_LIMIT),
        name="ffn",
    )(h2, g.reshape(1, d), wgu_bf16, wd_bf16, final_g.reshape(1, d))


LRU_ROWS = 216
HALO = 8


def _lru_kernel(x_ref, g_ref, cw_ref, cb_ref, gw_ref, gb_ref, lam_ref, o_ref,
                xs, a0, b0, a1, b1, *, T):
    W = LRU_BLOCK
    m = T // 8
    zeros8 = jnp.zeros((HALO, W), F32)
    xs[pl.ds(0, HALO), :] = zeros8
    xs[pl.ds(HALO + T, HALO), :] = zeros8

    def copy_body(c, carry):
        r0 = pl.multiple_of(c * LRU_ROWS, 8)
        xs[pl.ds(HALO + r0, LRU_ROWS), :] = x_ref[pl.ds(r0, LRU_ROWS), :]
        return carry

    lax.fori_loop(0, T // LRU_ROWS, copy_body, 0)

    cw = cw_ref[...]
    cb = cb_ref[...]
    gb = gb_ref[...]
    gw = gw_ref[...]
    nl = -lam_ref[...]
    sp = jnp.maximum(nl, 0.0) + jnp.log1p(jnp.exp(-jnp.abs(nl)))

    def gate_body(c, carry):
        r0 = pl.multiple_of(c * LRU_ROWS, 8)
        win = xs[pl.ds(r0, LRU_ROWS + 2 * HALO), :]
        xc = cb + cw[0:1] * win[HALO - 2:HALO - 2 + LRU_ROWS]
        xc = xc + cw[1:2] * win[HALO - 1:HALO - 1 + LRU_ROWS]
        xc = xc + cw[2:3] * win[HALO:HALO + LRU_ROWS]
        xc = xc + cw[3:4] * win[HALO + 1:HALO + 1 + LRU_ROWS]
        gates = jax.nn.sigmoid(jnp.dot(xc.astype(BF16), gw, preferred_element_type=F32) + gb)
        for d, (a_s, b_s) in enumerate(((a0, b0), (a1, b1))):
            r = gates[:, (2 * d) * W:(2 * d + 1) * W]
            i = gates[:, (2 * d + 1) * W:(2 * d + 2) * W]
            log_a = (-LRU_C * sp[d:d + 1]) * r
            a = jnp.exp(log_a)
            a_s[pl.ds(r0, LRU_ROWS), :] = a
            b_s[pl.ds(r0, LRU_ROWS), :] = jnp.sqrt(1.0 - a * a) * i * xc
        return carry

    lax.fori_loop(0, T // LRU_ROWS, gate_body, 0)

    def scan_body(j, carry):
        hf, pf, hb, pb = carry
        rf = pl.ds(j, 8, stride=m)
        rb = pl.ds(m - 1 - j, 8, stride=m)
        af = a0[rf, :]
        hf = af * hf + b0[rf, :]
        pf = af * pf
        b0[rf, :] = hf
        a0[rf, :] = pf
        ab = a1[rb, :]
        hb = ab * hb + b1[rb, :]
        pb = ab * pb
        b1[rb, :] = hb
        a1[rb, :] = pb
        return hf, pf, hb, pb

    z = jnp.zeros((8, W), F32)
    o = jnp.ones((8, W), F32)
    hf, pf, hb, pb = lax.fori_loop(0, m, scan_body, (z, o, z, o), unroll=2)

    row = lax.broadcasted_iota(jnp.int32, (8, W), 0)
    cf = z
    cbk = z
    for _ in range(7):
        cf = jnp.where(row == 0, 0.0, pltpu.roll(hf + pf * cf, 1, axis=0))
        cbk = jnp.where(row == 7, 0.0, pltpu.roll(hb + pb * cbk, 7, axis=0))

    def out_body(j, carry):
        rows = pl.ds(j, 8, stride=m)
        h = b0[rows, :] + a0[rows, :] * cf + b1[rows, :] + a1[rows, :] * cbk
        o_ref[rows, :] = h * jax.nn.gelu(g_ref[rows, :])
        return carry

    lax.fori_loop(0, m, out_body, 0, unroll=2)


def rglru_mixer(p, conv_w, conv_b, gate_w, gate_b, lam):
    B, T, _ = p.shape
    W = LRU_BLOCK
    assert T % LRU_ROWS == 0 and T % 8 == 0
    gw = jnp.transpose(gate_w, (2, 3, 0, 1, 4)).reshape(LRU_BLOCKS, W, 4 * W).astype(BF16)
    gb = jnp.transpose(gate_b.reshape(2, 2, LRU_BLOCKS, W), (2, 0, 1, 3)).reshape(LRU_BLOCKS, 1, 4 * W)
    xoff = OFF_LRU_X // W
    goff = OFF_LRU_G // W
    seq = pltpu.VMEM((T, W), F32)
    return pl.pallas_call(
        functools.partial(_lru_kernel, T=T),
        grid=(B, LRU_BLOCKS),
        in_specs=[
            pl.BlockSpec((None, T, W), lambda b, n: (b, 0, xoff + n)),
            pl.BlockSpec((None, T, W), lambda b, n: (b, 0, goff + n)),
            pl.BlockSpec((LRU_CONV, W), lambda b, n: (0, n)),
            pl.BlockSpec((1, W), lambda b, n: (0, n)),
            pl.BlockSpec((None, W, 4 * W), lambda b, n: (n, 0, 0)),
            pl.BlockSpec((None, 1, 4 * W), lambda b, n: (n, 0, 0)),
            pl.BlockSpec((2, W), lambda b, n: (0, n)),
        ],
        out_specs=pl.BlockSpec((None, T, W), lambda b, n: (b, 0, n)),
        out_shape=jax.ShapeDtypeStruct((B, T, LRU_WIDTH), F32),
        scratch_shapes=[pltpu.VMEM((T + 2 * HALO, W), F32), seq, seq, seq, seq],
        compiler_params=pltpu.CompilerParams(
            dimension_semantics=("parallel", "parallel"), vmem_limit_bytes=VMEM_LIMIT),
        name="rglru",
    )(p, p, conv_w, conv_b.reshape(1, -1), gw, gb, lam)


RET_ROWS = 216


def _ret_tables(T):
    C = RET_ROWS
    half = RET_DK // 2
    inv = ROPE_BASE ** (-np.linspace(0.0, 1.0, half, dtype=np.float32))
    ang = np.arange(T, dtype=np.float32)[:, None] * inv[None, :].astype(np.float32)
    cos = np.cos(ang.astype(np.float64)).astype(np.float32)
    sin = np.sin(ang.astype(np.float64)).astype(np.float32)
    log_gamma = np.log1p(-(2.0 ** (-5.0 - np.arange(RET_HEADS, dtype=np.float64))))
    idx = np.arange(C, dtype=np.float64)
    dmat = np.exp(np.abs(idx[:, None] - idx[None, :])[None] * log_gamma[:, None, None])
    sc = np.stack([np.exp((idx + 1.0)[:, None] * log_gamma[None]),
                   np.exp((C - 1.0 - idx)[:, None] * log_gamma[None]),
                   np.exp((C - idx)[:, None] * log_gamma[None]),
                   np.exp(idx[:, None] * log_gamma[None])])
    chunk_decay = np.exp(C * log_gamma)
    return (jnp.asarray(cos), jnp.asarray(sin), jnp.asarray(dmat, F32),
            jnp.asarray(sc, F32), [float(c) for c in chunk_decay])


def _rot(x, cos, sin):
    half = RET_DK // 2
    x1, x2 = x[:, :half], x[:, half:]
    return jnp.concatenate([x1 * cos - x2 * sin, x1 * sin + x2 * cos], axis=-1)


_TN = (((0,), (0,)), ((), ()))
_NT = (((1,), (1,)), ((), ()))


def _ret_fwd_kernel(q_ref, k_ref, v_ref, cos_ref, sin_ref, d_ref, sc_ref, o_ref, s_scr, *, decay):
    @pl.when(pl.program_id(1) == 0)
    def _():
        s_scr[...] = jnp.zeros_like(s_scr)

    cos = cos_ref[...]
    sin = sin_ref[...]
    for h in range(RET_HEADS):
        cols = slice(h * RET_DK, (h + 1) * RET_DK)
        q = _rot(q_ref[:, cols], cos, sin)
        k = _rot(k_ref[:, cols], cos, sin) * (RET_DK ** -0.5)
        v = v_ref[:, cols].astype(BF16)
        scores = lax.dot_general(q.astype(BF16), k.astype(BF16), _NT, preferred_element_type=F32)
        scores = scores * d_ref[h]
        y = jnp.dot(scores.astype(BF16), v, preferred_element_type=F32)
        state = s_scr[h]
        qx = (q * sc_ref[0, :, h:h + 1]).astype(BF16)
        y = y + jnp.dot(qx, state.astype(BF16), preferred_element_type=F32)
        o_ref[:, cols] = y
        kz = (k * sc_ref[1, :, h:h + 1]).astype(BF16)
        s_scr[h] = decay[h] * state + lax.dot_general(kz, v, _TN, preferred_element_type=F32)


def _ret_bwd_kernel(q_ref, k_ref, v_ref, g_ref, y_ref, cos_ref, sin_ref, sc_ref, o_ref, s_scr, *, decay):
    @pl.when(pl.program_id(1) == 0)
    def _():
        s_scr[...] = jnp.zeros_like(s_scr)

    cos = cos_ref[...]
    sin = sin_ref[...]
    for h in range(RET_HEADS):
        cols = slice(h * RET_DK, (h + 1) * RET_DK)
        q = _rot(q_ref[:, cols], cos, sin)
        k = _rot(k_ref[:, cols], cos, sin) * (RET_DK ** -0.5)
        v = v_ref[:, cols].astype(BF16)
        state = s_scr[h]
        qx = (q * sc_ref[2, :, h:h + 1]).astype(BF16)
        y = y_ref[:, cols] + jnp.dot(qx, state.astype(BF16), preferred_element_type=F32)
        kz = (k * sc_ref[3, :, h:h + 1]).astype(BF16)
        s_scr[h] = decay[h] * state + lax.dot_general(kz, v, _TN, preferred_element_type=F32)
        mu = jnp.mean(y, axis=-1, keepdims=True)
        yc = y - mu
        var = jnp.mean(yc * yc, axis=-1, keepdims=True)
        g = g_ref[:, cols]
        o_ref[:, cols] = g * jax.nn.sigmoid(g) * (yc * lax.rsqrt(var + EPS))


def retention_mixer(p):
    B, T, _ = p.shape
    C = RET_ROWS
    assert T % C == 0
    n = T // C
    cos, sin, dmat, sc, decay = _ret_tables(T)
    half = RET_DK // 2
    qb, kb, vb, gb = (o // RET_WIDTH for o in (OFF_Q, OFF_K, OFF_V, OFF_G))
    state = pltpu.VMEM((RET_HEADS, RET_DK, RET_DV), F32)
    params = pltpu.CompilerParams(dimension_semantics=("parallel", "arbitrary"), vmem_limit_bytes=VMEM_LIMIT)

    def col(blk, rev):
        if rev:
            return pl.BlockSpec((None, C, RET_WIDTH), lambda b, i: (b, n - 1 - i, blk))
        return pl.BlockSpec((None, C, RET_WIDTH), lambda b, i: (b, i, blk))

    def tab(rev):
        if rev:
            return pl.BlockSpec((C, half), lambda b, i: (n - 1 - i, 0))
        return pl.BlockSpec((C, half), lambda b, i: (i, 0))

    const3 = lambda b, i: (0, 0, 0)
    y1 = pl.pallas_call(
        functools.partial(_ret_fwd_kernel, decay=decay),
        grid=(B, n),
        in_specs=[col(qb, False), col(kb, False), col(vb, False), tab(False), tab(False),
                  pl.BlockSpec((RET_HEADS, C, C), const3), pl.BlockSpec((4, C, RET_HEADS), const3)],
        out_specs=col(0, False),
        out_shape=jax.ShapeDtypeStruct((B, T, RET_WIDTH), F32),
        scratch_shapes=[state],
        compiler_params=params,
        name="ret_fwd",
    )(p, p, p, cos, sin, dmat, sc)
    return pl.pallas_call(
        functools.partial(_ret_bwd_kernel, decay=decay),
        grid=(B, n),
        in_specs=[col(qb, True), col(kb, True), col(vb, True), col(gb, True), col(0, True),
                  tab(True), tab(True), pl.BlockSpec((4, C, RET_HEADS), const3)],
        out_specs=col(0, True),
        out_shape=jax.ShapeDtypeStruct((B, T, RET_WIDTH), F32),
        scratch_shapes=[state],
        compiler_params=params,
        name="ret_bwd",
    )(p, p, p, p, y1, cos, sin, sc)


FFT_N2 = 128
HY_ROWS = 216


def _fft_dims(T):
    r = -(-T // FFT_N2)
    r = -(-r // 8) * 8
    return r, 2 * r


def _cblock(z):
    return np.block([[z.real, -z.imag], [z.imag, z.real]])


def _fft_tables(T):
    r, n1 = _fft_dims(T)
    n2 = FFT_N2
    n = n1 * n2
    a1 = np.arange(n1, dtype=np.float64)
    a2 = np.arange(n2, dtype=np.float64)
    f1 = np.exp(-2j * np.pi * np.outer(a1, a1) / n1)
    m1 = _cblock(f1[:, :r])
    m1f = np.concatenate([f1.real, f1.imag], axis=0)
    m1inv = _cblock(np.conj(f1.T)[:r, :] / n)
    f2 = np.exp(-2j * np.pi * np.outer(a2, a2) / n2)
    tw = np.exp(-2j * np.pi * np.outer(a1, a2) / n)
    g = f2[None, :, :] * tw[:, None, :]
    gi = np.conj(np.transpose(g, (0, 2, 1)))
    gblk = np.stack([_cblock(g[k]) for k in range(n1)])
    giblk = np.stack([_cblock(gi[k]) for k in range(n1)])
    cast = lambda a: jnp.asarray(a, F32).astype(BF16)
    return cast(m1), cast(m1f), cast(m1inv), cast(gblk), cast(giblk)


def _short_conv_kernel(x_ref, w_ref, b_ref, o_ref, xs, *, T, Tp):
    W = x_ref.shape[-1]
    zeros8 = jnp.zeros((HALO, W), F32)
    xs[pl.ds(0, HALO), :] = zeros8
    xs[pl.ds(HALO + T, HALO), :] = zeros8

    def copy_body(c, carry):
        r0 = pl.multiple_of(c * HY_ROWS, 8)
        xs[pl.ds(HALO + r0, HY_ROWS), :] = x_ref[pl.ds(r0, HY_ROWS), :]
        return carry

    lax.fori_loop(0, T // HY_ROWS, copy_body, 0)
    w = w_ref[...]
    b = b_ref[...]

    def conv_body(c, carry):
        r0 = pl.multiple_of(c * HY_ROWS, 8)
        win = xs[pl.ds(r0, HY_ROWS + 2 * HALO), :]
        y = b + w[0:1] * win[HALO - 1:HALO - 1 + HY_ROWS]
        y = y + w[1:2] * win[HALO:HALO + HY_ROWS]
        y = y + w[2:3] * win[HALO + 1:HALO + 1 + HY_ROWS]
        o_ref[pl.ds(r0, HY_ROWS), :] = y
        return carry

    lax.fori_loop(0, T // HY_ROWS, conv_body, 0)
    o_ref[pl.ds(T, Tp - T), :] = jnp.zeros((Tp - T, W), F32)


def hyena_short_conv(p, conv_w, conv_b, Tp):
    B, T, _ = p.shape
    W = 128
    ncol = (HY_ORDER + 1) * HY_WIDTH
    return pl.pallas_call(
        functools.partial(_short_conv_kernel, T=T, Tp=Tp),
        grid=(B, ncol // W),
        in_specs=[
            pl.BlockSpec((None, T, W), lambda b, n: (b, 0, n)),
            pl.BlockSpec((HY_SHORT, W), lambda b, n: (0, n)),
            pl.BlockSpec((1, W), lambda b, n: (0, n)),
        ],
        out_specs=pl.BlockSpec((None, Tp, W), lambda b, n: (b, 0, n)),
        out_shape=jax.ShapeDtypeStruct((B, Tp, ncol), F32),
        scratch_shapes=[pltpu.VMEM((T + 2 * HALO, W), F32)],
        compiler_params=pltpu.CompilerParams(
            dimension_semantics=("parallel", "parallel"), vmem_limit_bytes=VMEM_LIMIT),
        name="hyena_short_conv",
    )(p, conv_w, conv_b.reshape(1, -1))


def _filter_tables(T):
    r, n1 = _fft_dims(T)
    n = n1 * FFT_N2
    idx = np.arange(n)
    fwd = idx < T
    bwd = idx > n - T
    lag = np.where(fwd, idx, np.where(bwd, n - idx, 0)).astype(np.float32)
    t_norm = lag / np.float32(max(T - 1, 1))
    bands = np.linspace(1e-4, HY_BANDS - 1, HY_BANDS, dtype=np.float32)
    ang = (np.float32(2.0 * math.pi / T) * lag[:, None]) * bands[None, :]
    feat = np.concatenate([t_norm[:, None], np.cos(ang), np.sin(ang)], axis=-1)
    valid = (fwd | bwd)[:, None]
    tab = np.zeros((n, FILT_COLS), np.float32)
    tab[:, :feat.shape[1]] = np.where(valid, feat, 0.0)
    tab[:, FILT_COLS - 3] = fwd
    tab[:, FILT_COLS - 2] = bwd
    tab[:, FILT_COLS - 1] = t_norm
    max_decay = math.log(HY_DECAY_TARGET) / HY_DECAY_PCT_SHORT
    min_decay = math.log(HY_DECAY_TARGET) / HY_DECAY_PCT_LONG
    deltas = np.abs(np.linspace(min_decay, max_decay, HY_WIDTH, dtype=np.float32))
    return jnp.asarray(tab), jnp.asarray(deltas).reshape(1, -1)


FILT_COLS = 40
FILT_ROWS = 512
_HI = lax.Precision.HIGHEST


def _filter_kernel(tab_ref, w1_ref, b1_ref, w2_ref, b2_ref, w3_ref, b3_ref, fr_ref, wo_ref, dl_ref, o_ref):
    tab = tab_ref[...]
    fr = fr_ref[...]
    hdn = jnp.sin(fr * (jnp.dot(tab, w1_ref[...], precision=_HI, preferred_element_type=F32) + b1_ref[...]))
    hdn = jnp.sin(fr * (jnp.dot(hdn, w2_ref[...], precision=_HI, preferred_element_type=F32) + b2_ref[...]))
    hdn = jnp.sin(fr * (jnp.dot(hdn, w3_ref[...], precision=_HI, preferred_element_type=F32) + b3_ref[...]))
    m_f = tab[:, FILT_COLS - 3:FILT_COLS - 2]
    m_b = tab[:, FILT_COLS - 2:FILT_COLS - 1]
    window = jnp.exp(-tab[:, FILT_COLS - 1:FILT_COLS] * dl_ref[...])
    for o in range(HY_ORDER):
        base = o * 2 * HY_WIDTH
        kf = jnp.dot(hdn, wo_ref[:, base:base + HY_WIDTH], precision=_HI, preferred_element_type=F32)
        kb = jnp.dot(hdn, wo_ref[:, base + HY_WIDTH:base + 2 * HY_WIDTH], precision=_HI,
                     preferred_element_type=F32)
        o_ref[o] = (m_f * kf + m_b * kb) * window


def hyena_filter(T, w1, b1, w2, b2, w3, b3, freq, wout):
    tab, deltas = _filter_tables(T)
    n = tab.shape[0]
    hid = w1.shape[1]
    w1p = jnp.zeros((FILT_COLS, hid), F32).at[:w1.shape[0]].set(w1)
    full = lambda i: (0, 0)
    vec = lambda a: a.reshape(1, -1)
    return pl.pallas_call(
        _filter_kernel,
        grid=(n // FILT_ROWS,),
        in_specs=[
            pl.BlockSpec((FILT_ROWS, FILT_COLS), lambda i: (i, 0)),
            pl.BlockSpec(w1p.shape, full), pl.BlockSpec((1, hid), full),
            pl.BlockSpec(w2.shape, full), pl.BlockSpec((1, hid), full),
            pl.BlockSpec(w3.shape, full), pl.BlockSpec((1, hid), full),
            pl.BlockSpec((1, hid), full),
            pl.BlockSpec(wout.shape, full),
            pl.BlockSpec((1, HY_WIDTH), full),
        ],
        out_specs=pl.BlockSpec((HY_ORDER, FILT_ROWS, HY_WIDTH), lambda i: (0, i, 0)),
        out_shape=jax.ShapeDtypeStruct((HY_ORDER, n, HY_WIDTH), F32),
        compiler_params=pltpu.CompilerParams(
            dimension_semantics=("parallel",), vmem_limit_bytes=VMEM_LIMIT),
        name="hyena_filter",
    )(tab, w1p, vec(b1), w2, vec(b2), w3, vec(b3), vec(freq), wout, deltas)


SUB = 8
LANES = 128


def _dft_rows_kernel(x_ref, m_ref, o_ref):
    nb, r = x_ref.shape[0], x_ref.shape[1]
    m = m_ref[...]
    for j in range(SUB):
        xj = x_ref[:, :, j, :].reshape(nb * r, LANES).astype(BF16)
        y = jnp.dot(m, xj, preferred_element_type=F32)
        o_ref[:, :, j, :] = y.reshape(o_ref.shape[0], o_ref.shape[1], LANES)


def dft_stage1(x, m, col_block, n1):
    G, nb, r, n2, _ = x.shape
    C = HY_WIDTH
    ct = C // LANES
    return pl.pallas_call(
        _dft_rows_kernel,
        grid=(G, n2 // SUB, ct),
        in_specs=[
            pl.BlockSpec((None, nb, r, SUB, LANES), lambda g, j, c: (g, 0, 0, j, col_block * ct + c)),
            pl.BlockSpec(m.shape, lambda g, j, c: (0, 0)),
        ],
        out_specs=pl.BlockSpec((None, 2, n1, SUB, LANES), lambda g, j, c: (g, 0, 0, j, c)),
        out_shape=jax.ShapeDtypeStruct((G, 2, n1, n2, C), F32),
        compiler_params=pltpu.CompilerParams(
            dimension_semantics=("parallel", "parallel", "parallel"), vmem_limit_bytes=VMEM_LIMIT),
        name="dft_stage1",
    )(x, m)


def _filter_spec_kernel(y_ref, g_ref, o_ref):
    y = y_ref[...].reshape(2 * FFT_N2, -1).astype(BF16)
    o_ref[...] = jnp.dot(g_ref[...], y, preferred_element_type=F32).reshape(o_ref.shape)


def filter_spectrum(yf, gblk):
    G, _, n1, n2, C = yf.shape
    blk = pl.BlockSpec((None, 2, None, n2, C), lambda g, k: (g, 0, k, 0, 0))
    return pl.pallas_call(
        _filter_spec_kernel,
        grid=(G, n1),
        in_specs=[blk, pl.BlockSpec((None, 2 * n2, 2 * n2), lambda g, k: (k, 0, 0))],
        out_specs=blk,
        out_shape=jax.ShapeDtypeStruct(yf.shape, F32),
        compiler_params=pltpu.CompilerParams(
            dimension_semantics=("parallel", "parallel"), vmem_limit_bytes=VMEM_LIMIT),
        name="filter_spectrum",
    )(yf, gblk)


def _spectral_kernel(y_ref, kf_ref, g_ref, gi_ref, o_ref):
    y = y_ref[...].reshape(2 * FFT_N2, -1).astype(BF16)
    z = jnp.dot(g_ref[...], y, preferred_element_type=F32)
    zr, zi = z[:FFT_N2], z[FFT_N2:]
    kr, ki = kf_ref[0], kf_ref[1]
    w = jnp.concatenate([zr * kr - zi * ki, zr * ki + zi * kr], axis=0).astype(BF16)
    v = jnp.dot(gi_ref[...], w, preferred_element_type=F32)
    o_ref[...] = v.reshape(o_ref.shape).astype(o_ref.dtype)


def spectral_stage(y, kf, order, gblk, giblk):
    _, n1, n2, C = y.shape
    blk = pl.BlockSpec((2, None, n2, C), lambda k: (0, k, 0, 0))
    mat = pl.BlockSpec((None, 2 * n2, 2 * n2), lambda k: (k, 0, 0))
    return pl.pallas_call(
        _spectral_kernel,
        grid=(n1,),
        in_specs=[blk, pl.BlockSpec((None, 2, None, n2, C), lambda k: (order, 0, k, 0, 0)), mat, mat],
        out_specs=blk,
        out_shape=jax.ShapeDtypeStruct(y.shape, F32),
        compiler_params=pltpu.CompilerParams(
            dimension_semantics=("parallel",), vmem_limit_bytes=VMEM_LIMIT),
        name="hyena_spectral",
    )(y, kf, gblk, giblk)


def _idft_gate_kernel(v_ref, m_ref, gate_ref, z_ref, skip_ref, o_ref, conv_scr):
    rows_in = v_ref.shape[0] * v_ref.shape[1]
    m = m_ref[...]
    for j in range(SUB):
        vj = v_ref[:, :, j, :].reshape(rows_in, LANES).astype(BF16)
        conv = jnp.dot(m, vj, preferred_element_type=F32)
        conv_scr[:, :, j, :] = conv.reshape(o_ref.shape[0], o_ref.shape[1], LANES)
    o_ref[...] = gate_ref[...] * (conv_scr[...] + z_ref[...] * skip_ref[...])


def idft_gate(v, m1inv, uc5, gate_block, z5, z_block, skip):
    _, n1, n2, C = v.shape
    B, r = uc5.shape[0], uc5.shape[1]
    ct = C // LANES
    rows = lambda blk: pl.BlockSpec((B, r, SUB, LANES), lambda j, c: (0, 0, j, blk * ct + c))
    return pl.pallas_call(
        _idft_gate_kernel,
        grid=(n2 // SUB, ct),
        in_specs=[
            pl.BlockSpec((2, n1, SUB, LANES), lambda j, c: (0, 0, j, c)),
            pl.BlockSpec(m1inv.shape, lambda j, c: (0, 0)),
            rows(gate_block), rows(z_block),
            pl.BlockSpec((1, LANES), lambda j, c: (0, c)),
        ],
        out_specs=rows(0),
        out_shape=jax.ShapeDtypeStruct((B, r, n2, C), F32),
        scratch_shapes=[pltpu.VMEM((B, r, SUB, LANES), F32)],
        compiler_params=pltpu.CompilerParams(
            dimension_semantics=("parallel", "parallel"), vmem_limit_bytes=VMEM_LIMIT),
        name="hyena_idft_gate",
    )(v, m1inv, uc5, z5, skip.reshape(1, C))


def hyena_mixer(p, conv_w, conv_b, fw1, fb1, fw2, fb2, fw3, fb3, freq, wout, skip):
    B, T, _ = p.shape
    assert B == 2, "the two batch rows are packed as one complex sequence"
    r, n1 = _fft_dims(T)
    n2 = FFT_N2
    Tp = r * n2
    m1, m1f, m1inv, gblk, giblk = _fft_tables(T)
    kc = hyena_filter(T, fw1, fb1, fw2, fb2, fw3, fb3, freq, wout)
    yf = dft_stage1(kc.reshape(HY_ORDER, 1, n1, n2, HY_WIDTH), m1f, 0, n1)
    kf = filter_spectrum(yf, gblk)
    uc = hyena_short_conv(p, conv_w, conv_b, Tp)
    uc4 = uc.reshape(B, r, n2, (HY_ORDER + 1) * HY_WIDTH)
    z4, zblk = uc4, HY_ORDER
    for o in range(HY_ORDER):
        y = dft_stage1(z4[None], m1, zblk, n1)[0]
        v = spectral_stage(y, kf, o, gblk, giblk)
        z4, zblk = idft_gate(v, m1inv, uc4, o, z4, zblk, skip[o]), 0
    return z4.reshape(B, Tp, HY_WIDTH)


def kernel(x, meta_tokens, norm1_g, w_in, hy_conv_w, hy_conv_b, hy_f_w1, hy_f_b1, hy_f_w2, hy_f_b2, hy_f_w3, hy_f_b3, hy_f_freq, hy_f_wout, hy_skip, lru_conv_w, lru_conv_b, lru_gate_w, lru_gate_b, lru_lambda, w_branch, w_out, norm2_g, ffn_w_gu, ffn_w_down, final_g):
    B = x.shape[0]
    meta = jnp.broadcast_to(meta_tokens[None].astype(x.dtype), (B, N_META, D_MODEL))
    h = jnp.concatenate([meta, x], axis=1)
    T = h.shape[1]
    for l in range(DEPTH):
        p = norm_proj(h.reshape(B * T, D_MODEL), norm1_g[l], w_in[l].astype(BF16)).reshape(B, T, D_IN)
        y_hy = hyena_mixer(p, hy_conv_w[l], hy_conv_b[l], hy_f_w1[l], hy_f_b1[l], hy_f_w2[l], hy_f_b2[l],
                           hy_f_w3[l], hy_f_b3[l], hy_f_freq[l], hy_f_wout[l], hy_skip[l])
        y_lru = rglru_mixer(p, lru_conv_w[l], lru_conv_b[l], lru_gate_w[l], lru_gate_b[l], lru_lambda[l])
        y_ret = retention_mixer(p)
        h = merge_out(h, y_hy, y_lru, y_ret, p, w_branch[l].astype(BF16), w_out[l].astype(BF16))
        h = ffn(h.reshape(B * T, D_MODEL), norm2_g[l], ffn_w_gu[l].astype(BF16), ffn_w_down[l].astype(BF16),
                final_g, final_norm=(l == DEPTH - 1)).reshape(B, T, D_MODEL)
    return h[:, N_META:]
```

```python
import functools
import math

import jax
import jax.numpy as jnp
import numpy as np
from jax import lax
from jax.experimental import pallas as pl
from jax.experimental.pallas import tpu as pltpu

F32 = jnp.float32
BF16 = jnp.bfloat16

D_MODEL = 1024
N_META = 16
EPS = 1e-6
DEPTH = 2

HY_WIDTH = D_MODEL
HY_ORDER = 2
HY_SHORT = 3
HY_BANDS = 16
HY_DECAY_PCT_SHORT = 0.3
HY_DECAY_PCT_LONG = 1.5
HY_DECAY_TARGET = 1e-2

LRU_WIDTH = D_MODEL
LRU_BLOCKS = 8
LRU_BLOCK = LRU_WIDTH // LRU_BLOCKS
LRU_CONV = 4
LRU_C = 8.0

RET_HEADS = 4
RET_DK = D_MODEL // RET_HEADS
RET_DV = D_MODEL // RET_HEADS
RET_WIDTH = RET_HEADS * RET_DV
RET_CHUNK = 128
ROPE_BASE = 10000.0

N_BRANCH = 3
D_FF = ((8 * D_MODEL + 3 * 256 - 1) // (3 * 256)) * 256

OFF_LRU_X = HY_ORDER * HY_WIDTH + HY_WIDTH
OFF_LRU_G = OFF_LRU_X + LRU_WIDTH
OFF_Q = OFF_LRU_G + LRU_WIDTH
OFF_K = OFF_Q + RET_HEADS * RET_DK
OFF_V = OFF_K + RET_HEADS * RET_DK
OFF_G = OFF_V + RET_WIDTH
OFF_MERGE = OFF_G + RET_WIDTH
D_IN = OFF_MERGE + N_BRANCH * D_MODEL

ROW_TILE = 912
MERGE_ROW_TILE = 432
VMEM_LIMIT = 56 * 1024 * 1024


def _rms(x, g):
    ms = jnp.mean(x * x, axis=-1, keepdims=True)
    return x * lax.rsqrt(ms + EPS) * g


def _sigmoid(x):
    return 0.5 * jnp.tanh(0.5 * x) + 0.5


def _norm_proj_kernel(h_ref, g_ref, w_ref, o_ref, u_scr):
    @pl.when(pl.program_id(1) == 0)
    def _():
        u_scr[...] = _rms(h_ref[...], g_ref[...]).astype(BF16)

    o_ref[...] = jnp.dot(u_scr[...], w_ref[...], preferred_element_type=F32)


def norm_proj(h2, g, w_bf16, tn=1024):
    m, d = h2.shape
    n = w_bf16.shape[1]
    return pl.pallas_call(
        _norm_proj_kernel,
        grid=(m // ROW_TILE, n // tn),
        in_specs=[
            pl.BlockSpec((ROW_TILE, d), lambda i, j: (i, 0)),
            pl.BlockSpec((1, d), lambda i, j: (0, 0)),
            pl.BlockSpec((d, tn), lambda i, j: (0, j)),
        ],
        out_specs=pl.BlockSpec((ROW_TILE, tn), lambda i, j: (i, j)),
        out_shape=jax.ShapeDtypeStruct((m, n), F32),
        scratch_shapes=[pltpu.VMEM((ROW_TILE, d), BF16)],
        compiler_params=pltpu.CompilerParams(
            dimension_semantics=("parallel", "arbitrary"), vmem_limit_bytes=VMEM_LIMIT),
        name="norm_proj",
    )(h2, g.reshape(1, d), w_bf16)


def _merge_kernel(h_ref, yh_ref, yl_ref, yr_ref, gate_ref, wb_ref, wo_ref, o_ref):
    merged = None
    for bi, y_ref in enumerate((yh_ref, yl_ref, yr_ref)):
        proj = jnp.dot(y_ref[...].astype(BF16), wb_ref[bi], preferred_element_type=F32)
        term = _sigmoid(gate_ref[:, bi * D_MODEL:(bi + 1) * D_MODEL]) * proj
        merged = term if merged is None else merged + term
    o_ref[...] = h_ref[...] + jnp.dot(merged.astype(BF16), wo_ref[...], preferred_element_type=F32)


def merge_out(h, y_hy, y_lru, y_ret, p, wb_bf16, wo_bf16):
    B, T, d = h.shape
    tm = MERGE_ROW_TILE
    assert T % tm == 0
    row = lambda b, i: (b, i, 0)
    gate_blk = OFF_MERGE // (N_BRANCH * D_MODEL)
    blk = pl.BlockSpec((None, tm, d), row)
    return pl.pallas_call(
        _merge_kernel,
        grid=(B, T // tm),
        in_specs=[
            blk, blk, blk, blk,
            pl.BlockSpec((None, tm, N_BRANCH * D_MODEL), lambda b, i: (b, i, gate_blk)),
            pl.BlockSpec((N_BRANCH, d, d), lambda b, i: (0, 0, 0), pipeline_mode=pl.Buffered(1)),
            pl.BlockSpec((d, d), lambda b, i: (0, 0), pipeline_mode=pl.Buffered(1)),
        ],
        out_specs=blk,
        out_shape=jax.ShapeDtypeStruct((B, T, d), F32),
        compiler_params=pltpu.CompilerParams(
            dimension_semantics=("parallel", "parallel"), vmem_limit_bytes=VMEM_LIMIT),
        name="merge_out",
    )(h, y_hy, y_lru, y_ret, p, wb_bf16, wo_bf16)


FF_CHUNK = 256


def _ffn_kernel(h_ref, g_ref, wgu_ref, wd_ref, fg_ref, o_ref, v_scr, acc_scr, *, final_norm):
    h = h_ref[...]
    v_scr[...] = _rms(h, g_ref[...]).astype(BF16)
    acc_scr[...] = h

    def body(c, carry):
        off = pl.multiple_of(c * FF_CHUNK, FF_CHUNK)
        v = v_scr[...]
        gate = jnp.dot(v, wgu_ref[:, pl.ds(off, FF_CHUNK)], preferred_element_type=F32)
        up = jnp.dot(v, wgu_ref[:, pl.ds(D_FF + off, FF_CHUNK)], preferred_element_type=F32)
        hid = (gate * _sigmoid(gate) * up).astype(BF16)
        acc_scr[...] += jnp.dot(hid, wd_ref[pl.ds(off, FF_CHUNK), :], preferred_element_type=F32)
        return carry

    lax.fori_loop(0, D_FF // FF_CHUNK, body, 0)
    out = acc_scr[...]
    if final_norm:
        out = _rms(out, fg_ref[...])
    o_ref[...] = out


def ffn(h2, g, wgu_bf16, wd_bf16, final_g, final_norm):
    m, d = h2.shape
    row = lambda i: (i, 0)
    return pl.pallas_call(
        functools.partial(_ffn_kernel, final_norm=final_norm),
        grid=(m // ROW_TILE,),
        in_specs=[
            pl.BlockSpec((ROW_TILE, d), row),
            pl.BlockSpec((1, d), lambda i: (0, 0)),
            pl.BlockSpec((d, 2 * D_FF), lambda i: (0, 0), pipeline_mode=pl.Buffered(1)),
            pl.BlockSpec((D_FF, d), lambda i: (0, 0), pipeline_mode=pl.Buffered(1)),
            pl.BlockSpec((1, d), lambda i: (0, 0)),
        ],
        out_specs=pl.BlockSpec((ROW_TILE, d), row),
        out_shape=jax.ShapeDtypeStruct((m, d), F32),
        scratch_shapes=[pltpu.VMEM((ROW_TILE, d), BF16), pltpu.VMEM((ROW_TILE, d), F32)],
        compiler_params=pltpu.CompilerParams(
            dimension_semantics=("parallel",), vmem_limit_bytes=VMEM_LIMIT),
        name="ffn",
    )(h2, g.reshape(1, d), wgu_bf16, wd_bf16, final_g.reshape(1, d))


LRU_ROWS = 216
HALO = 8


def _lru_kernel(x_ref, g_ref, cw_ref, cb_ref, gw_ref, gb_ref, lam_ref, o_ref,
                xs, a0, b0, a1, b1, *, T):
    W = LRU_BLOCK
    m = T // 8
    zeros8 = jnp.zeros((HALO, W), F32)
    xs[pl.ds(0, HALO), :] = zeros8
    xs[pl.ds(HALO + T, HALO), :] = zeros8

    def copy_body(c, carry):
        r0 = pl.multiple_of(c * LRU_ROWS, 8)
        xs[pl.ds(HALO + r0, LRU_ROWS), :] = x_ref[pl.ds(r0, LRU_ROWS), :]
        return carry

    lax.fori_loop(0, T // LRU_ROWS, copy_body, 0)

    cw = cw_ref[...]
    cb = cb_ref[...]
    gb = gb_ref[...]
    gw = gw_ref[...]
    nl = -lam_ref[...]
    sp = jnp.maximum(nl, 0.0) + jnp.log1p(jnp.exp(-jnp.abs(nl)))

    def gate_body(c, carry):
        r0 = pl.multiple_of(c * LRU_ROWS, 8)
        win = xs[pl.ds(r0, LRU_ROWS + 2 * HALO), :]
        xc = cb + cw[0:1] * win[HALO - 2:HALO - 2 + LRU_ROWS]
        xc = xc + cw[1:2] * win[HALO - 1:HALO - 1 + LRU_ROWS]
        xc = xc + cw[2:3] * win[HALO:HALO + LRU_ROWS]
        xc = xc + cw[3:4] * win[HALO + 1:HALO + 1 + LRU_ROWS]
        gates = _sigmoid(jnp.dot(xc.astype(BF16), gw, preferred_element_type=F32) + gb)
        for d, (a_s, b_s) in enumerate(((a0, b0), (a1, b1))):
            r = gates[:, (2 * d) * W:(2 * d + 1) * W]
            i = gates[:, (2 * d + 1) * W:(2 * d + 2) * W]
            log_a = (-LRU_C * sp[d:d + 1]) * r
            a = jnp.exp(log_a)
            a_s[pl.ds(r0, LRU_ROWS), :] = a
            b_s[pl.ds(r0, LRU_ROWS), :] = jnp.sqrt(1.0 - a * a) * i * xc
        return carry

    lax.fori_loop(0, T // LRU_ROWS, gate_body, 0)

    def scan_body(j, carry):
        hf, pf, hb, pb = carry
        rf = pl.ds(j, 8, stride=m)
        rb = pl.ds(m - 1 - j, 8, stride=m)
        af = a0[rf, :]
        hf = af * hf + b0[rf, :]
        pf = af * pf
        b0[rf, :] = hf
        a0[rf, :] = pf
        ab = a1[rb, :]
        hb = ab * hb + b1[rb, :]
        pb = ab * pb
        b1[rb, :] = hb
        a1[rb, :] = pb
        return hf, pf, hb, pb

    z = jnp.zeros((8, W), F32)
    o = jnp.ones((8, W), F32)
    hf, pf, hb, pb = lax.fori_loop(0, m, scan_body, (z, o, z, o), unroll=2)

    row = lax.broadcasted_iota(jnp.int32, (8, W), 0)
    cf = z
    cbk = z
    for _ in range(7):
        cf = jnp.where(row == 0, 0.0, pltpu.roll(hf + pf * cf, 1, axis=0))
        cbk = jnp.where(row == 7, 0.0, pltpu.roll(hb + pb * cbk, 7, axis=0))

    def out_body(j, carry):
        rows = pl.ds(j, 8, stride=m)
        h = b0[rows, :] + a0[rows, :] * cf + b1[rows, :] + a1[rows, :] * cbk
        o_ref[rows, :] = h * jax.nn.gelu(g_ref[rows, :])
        return carry

    lax.fori_loop(0, m, out_body, 0, unroll=2)


def rglru_mixer(p, conv_w, conv_b, gate_w, gate_b, lam):
    B, T, _ = p.shape
    W = LRU_BLOCK
    assert T % LRU_ROWS == 0 and T % 8 == 0
    gw = jnp.transpose(gate_w, (2, 3, 0, 1, 4)).reshape(LRU_BLOCKS, W, 4 * W).astype(BF16)
    gb = jnp.transpose(gate_b.reshape(2, 2, LRU_BLOCKS, W), (2, 0, 1, 3)).reshape(LRU_BLOCKS, 1, 4 * W)
    xoff = OFF_LRU_X // W
    goff = OFF_LRU_G // W
    seq = pltpu.VMEM((T, W), F32)
    return pl.pallas_call(
        functools.partial(_lru_kernel, T=T),
        grid=(B, LRU_BLOCKS),
        in_specs=[
            pl.BlockSpec((None, T, W), lambda b, n: (b, 0, xoff + n)),
            pl.BlockSpec((None, T, W), lambda b, n: (b, 0, goff + n)),
            pl.BlockSpec((LRU_CONV, W), lambda b, n: (0, n)),
            pl.BlockSpec((1, W), lambda b, n: (0, n)),
            pl.BlockSpec((None, W, 4 * W), lambda b, n: (n, 0, 0)),
            pl.BlockSpec((None, 1, 4 * W), lambda b, n: (n, 0, 0)),
            pl.BlockSpec((2, W), lambda b, n: (0, n)),
        ],
        out_specs=pl.BlockSpec((None, T, W), lambda b, n: (b, 0, n)),
        out_shape=jax.ShapeDtypeStruct((B, T, LRU_WIDTH), F32),
        scratch_shapes=[pltpu.VMEM((T + 2 * HALO, W), F32), seq, seq, seq, seq],
        compiler_params=pltpu.CompilerParams(
            dimension_semantics=("parallel", "parallel"), vmem_limit_bytes=VMEM_LIMIT),
        name="rglru",
    )(p, p, conv_w, conv_b.reshape(1, -1), gw, gb, lam)


RET_ROWS = 216


def _ret_tables(T):
    C = RET_ROWS
    half = RET_DK // 2
    inv = ROPE_BASE ** (-np.linspace(0.0, 1.0, half, dtype=np.float32))
    ang = np.arange(T, dtype=np.float32)[:, None] * inv[None, :].astype(np.float32)
    cos = np.cos(ang.astype(np.float64)).astype(np.float32)
    sin = np.sin(ang.astype(np.float64)).astype(np.float32)
    log_gamma = np.log1p(-(2.0 ** (-5.0 - np.arange(RET_HEADS, dtype=np.float64))))
    idx = np.arange(C, dtype=np.float64)
    dmat = np.exp(np.abs(idx[:, None] - idx[None, :])[None] * log_gamma[:, None, None])
    sc = np.stack([np.exp((idx + 1.0)[:, None] * log_gamma[None]),
                   np.exp((C - 1.0 - idx)[:, None] * log_gamma[None]),
                   np.exp((C - idx)[:, None] * log_gamma[None]),
                   np.exp(idx[:, None] * log_gamma[None])])
    chunk_decay = np.exp(C * log_gamma)
    return (jnp.asarray(cos), jnp.asarray(sin), jnp.asarray(dmat, F32),
            jnp.asarray(sc, F32), [float(c) for c in chunk_decay])


def _rot(x, cos, sin):
    half = RET_DK // 2
    x1, x2 = x[:, :half], x[:, half:]
    return jnp.concatenate([x1 * cos - x2 * sin, x1 * sin + x2 * cos], axis=-1)


_TN = (((0,), (0,)), ((), ()))
_NT = (((1,), (1,)), ((), ()))


def _ret_fwd_kernel(q_ref, k_ref, v_ref, cos_ref, sin_ref, d_ref, sc_ref, o_ref, s_scr, *, decay):
    @pl.when(pl.program_id(1) == 0)
    def _():
        s_scr[...] = jnp.zeros_like(s_scr)

    cos = cos_ref[...]
    sin = sin_ref[...]
    for h in range(RET_HEADS):
        cols = slice(h * RET_DK, (h + 1) * RET_DK)
        q = _rot(q_ref[:, cols], cos, sin)
        k = _rot(k_ref[:, cols], cos, sin) * (RET_DK ** -0.5)
        v = v_ref[:, cols].astype(BF16)
        scores = lax.dot_general(q.astype(BF16), k.astype(BF16), _NT, preferred_element_type=F32)
        scores = scores * d_ref[h]
        y = jnp.dot(scores.astype(BF16), v, preferred_element_type=F32)
        state = s_scr[h]
        qx = (q * sc_ref[0, :, h:h + 1]).astype(BF16)
        y = y + jnp.dot(qx, state.astype(BF16), preferred_element_type=F32)
        o_ref[:, cols] = y
        kz = (k * sc_ref[1, :, h:h + 1]).astype(BF16)
        s_scr[h] = decay[h] * state + lax.dot_general(kz, v, _TN, preferred_element_type=F32)


def _ret_bwd_kernel(q_ref, k_ref, v_ref, g_ref, y_ref, cos_ref, sin_ref, sc_ref, o_ref, s_scr, *, decay):
    @pl.when(pl.program_id(1) == 0)
    def _():
        s_scr[...] = jnp.zeros_like(s_scr)

    cos = cos_ref[...]
    sin = sin_ref[...]
    for h in range(RET_HEADS):
        cols = slice(h * RET_DK, (h + 1) * RET_DK)
        q = _rot(q_ref[:, cols], cos, sin)
        k = _rot(k_ref[:, cols], cos, sin) * (RET_DK ** -0.5)
        v = v_ref[:, cols].astype(BF16)
        state = s_scr[h]
        qx = (q * sc_ref[2, :, h:h + 1]).astype(BF16)
        y = y_ref[:, cols] + jnp.dot(qx, state.astype(BF16), preferred_element_type=F32)
        kz = (k * sc_ref[3, :, h:h + 1]).astype(BF16)
        s_scr[h] = decay[h] * state + lax.dot_general(kz, v, _TN, preferred_element_type=F32)
        mu = jnp.mean(y, axis=-1, keepdims=True)
        yc = y - mu
        var = jnp.mean(yc * yc, axis=-1, keepdims=True)
        g = g_ref[:, cols]
        o_ref[:, cols] = g * _sigmoid(g) * (yc * lax.rsqrt(var + EPS))


def retention_mixer(p):
    B, T, _ = p.shape
    C = RET_ROWS
    assert T % C == 0
    n = T // C
    cos, sin, dmat, sc, decay = _ret_tables(T)
    half = RET_DK // 2
    qb, kb, vb, gb = (o // RET_WIDTH for o in (OFF_Q, OFF_K, OFF_V, OFF_G))
    state = pltpu.VMEM((RET_HEADS, RET_DK, RET_DV), F32)
    params = pltpu.CompilerParams(dimension_semantics=("parallel", "arbitrary"), vmem_limit_bytes=VMEM_LIMIT)

    def col(blk, rev):
        if rev:
            return pl.BlockSpec((None, C, RET_WIDTH), lambda b, i: (b, n - 1 - i, blk))
        return pl.BlockSpec((None, C, RET_WIDTH), lambda b, i: (b, i, blk))

    def tab(rev):
        if rev:
            return pl.BlockSpec((C, half), lambda b, i: (n - 1 - i, 0))
        return pl.BlockSpec((C, half), lambda b, i: (i, 0))

    const3 = lambda b, i: (0, 0, 0)
    y1 = pl.pallas_call(
        functools.partial(_ret_fwd_kernel, decay=decay),
        grid=(B, n),
        in_specs=[col(qb, False), col(kb, False), col(vb, False), tab(False), tab(False),
                  pl.BlockSpec((RET_HEADS, C, C), const3), pl.BlockSpec((4, C, RET_HEADS), const3)],
        out_specs=col(0, False),
        out_shape=jax.ShapeDtypeStruct((B, T, RET_WIDTH), F32),
        scratch_shapes=[state],
        compiler_params=params,
        name="ret_fwd",
    )(p, p, p, cos, sin, dmat, sc)
    return pl.pallas_call(
        functools.partial(_ret_bwd_kernel, decay=decay),
        grid=(B, n),
        in_specs=[col(qb, True), col(kb, True), col(vb, True), col(gb, True), col(0, True),
                  tab(True), tab(True), pl.BlockSpec((4, C, RET_HEADS), const3)],
        out_specs=col(0, True),
        out_shape=jax.ShapeDtypeStruct((B, T, RET_WIDTH), F32),
        scratch_shapes=[state],
        compiler_params=params,
        name="ret_bwd",
    )(p, p, p, p, y1, cos, sin, sc)


FFT_N2 = 128
HY_ROWS = 216


def _fft_dims(T):
    r = -(-T // FFT_N2)
    r = -(-r // 8) * 8
    return r, 2 * r


def _cblock(z):
    return np.block([[z.real, -z.imag], [z.imag, z.real]])


def _fft_tables(T):
    r, n1 = _fft_dims(T)
    n2 = FFT_N2
    n = n1 * n2
    a1 = np.arange(n1, dtype=np.float64)
    a2 = np.arange(n2, dtype=np.float64)
    f1 = np.exp(-2j * np.pi * np.outer(a1, a1) / n1)
    m1 = _cblock(f1[:, :r])
    m1f = np.concatenate([f1.real, f1.imag], axis=0)
    m1inv = _cblock(np.conj(f1.T)[:r, :] / n)
    f2 = np.exp(-2j * np.pi * np.outer(a2, a2) / n2)
    tw = np.exp(-2j * np.pi * np.outer(a1, a2) / n)
    g = f2[None, :, :] * tw[:, None, :]
    gi = np.conj(np.transpose(g, (0, 2, 1)))
    gblk = np.stack([_cblock(g[k]) for k in range(n1)])
    giblk = np.stack([_cblock(gi[k]) for k in range(n1)])
    cast = lambda a: jnp.asarray(a, F32).astype(BF16)
    return cast(m1), cast(m1f), cast(m1inv), cast(gblk), cast(giblk)


def _short_conv_kernel(x_ref, w_ref, b_ref, o_ref, xs, *, T, Tp):
    W = x_ref.shape[-1]
    zeros8 = jnp.zeros((HALO, W), F32)
    xs[pl.ds(0, HALO), :] = zeros8
    xs[pl.ds(HALO + T, HALO), :] = zeros8

    def copy_body(c, carry):
        r0 = pl.multiple_of(c * HY_ROWS, 8)
        xs[pl.ds(HALO + r0, HY_ROWS), :] = x_ref[pl.ds(r0, HY_ROWS), :]
        return carry

    lax.fori_loop(0, T // HY_ROWS, copy_body, 0)
    w = w_ref[...]
    b = b_ref[...]

    def conv_body(c, carry):
        r0 = pl.multiple_of(c * HY_ROWS, 8)
        win = xs[pl.ds(r0, HY_ROWS + 2 * HALO), :]
        y = b + w[0:1] * win[HALO - 1:HALO - 1 + HY_ROWS]
        y = y + w[1:2] * win[HALO:HALO + HY_ROWS]
        y = y + w[2:3] * win[HALO + 1:HALO + 1 + HY_ROWS]
        o_ref[pl.ds(r0, HY_ROWS), :] = y
        return carry

    lax.fori_loop(0, T // HY_ROWS, conv_body, 0)
    o_ref[pl.ds(T, Tp - T), :] = jnp.zeros((Tp - T, W), F32)


def hyena_short_conv(p, conv_w, conv_b, Tp):
    B, T, _ = p.shape
    W = 128
    ncol = (HY_ORDER + 1) * HY_WIDTH
    return pl.pallas_call(
        functools.partial(_short_conv_kernel, T=T, Tp=Tp),
        grid=(B, ncol // W),
        in_specs=[
            pl.BlockSpec((None, T, W), lambda b, n: (b, 0, n)),
            pl.BlockSpec((HY_SHORT, W), lambda b, n: (0, n)),
            pl.BlockSpec((1, W), lambda b, n: (0, n)),
        ],
        out_specs=pl.BlockSpec((None, Tp, W), lambda b, n: (b, 0, n)),
        out_shape=jax.ShapeDtypeStruct((B, Tp, ncol), F32),
        scratch_shapes=[pltpu.VMEM((T + 2 * HALO, W), F32)],
        compiler_params=pltpu.CompilerParams(
            dimension_semantics=("parallel", "parallel"), vmem_limit_bytes=VMEM_LIMIT),
        name="hyena_short_conv",
    )(p, conv_w, conv_b.reshape(1, -1))


def _filter_tables(T):
    r, n1 = _fft_dims(T)
    n = n1 * FFT_N2
    idx = np.arange(n)
    fwd = idx < T
    bwd = idx > n - T
    lag = np.where(fwd, idx, np.where(bwd, n - idx, 0)).astype(np.float32)
    t_norm = lag / np.float32(max(T - 1, 1))
    bands = np.linspace(1e-4, HY_BANDS - 1, HY_BANDS, dtype=np.float32)
    ang = (np.float32(2.0 * math.pi / T) * lag[:, None]) * bands[None, :]
    feat = np.concatenate([t_norm[:, None], np.cos(ang), np.sin(ang)], axis=-1)
    valid = (fwd | bwd)[:, None]
    tab = np.zeros((FILT_COLS, n), np.float32)
    tab[:feat.shape[1]] = np.where(valid, feat, 0.0).T
    col = np.stack([fwd, bwd, t_norm], axis=1).astype(np.float32)
    max_decay = math.log(HY_DECAY_TARGET) / HY_DECAY_PCT_SHORT
    min_decay = math.log(HY_DECAY_TARGET) / HY_DECAY_PCT_LONG
    deltas = np.abs(np.linspace(min_decay, max_decay, HY_WIDTH, dtype=np.float32))
    return jnp.asarray(tab), jnp.asarray(col), jnp.asarray(deltas).reshape(1, -1)


FILT_COLS = 40
FILT_ROWS = 512
_HI = lax.Precision.HIGHEST


def _filter_kernel(tab_ref, col_ref, w1_ref, b1_ref, w2_ref, b2_ref, w3_ref, b3_ref, fr_ref, wo_ref, dl_ref,
                   o_ref, *, n_fwd, first_bwd):
    i = pl.program_id(0)

    def taps(direction):
        fr = fr_ref[...]
        hdn = tab_ref[...]
        for w_ref, b_ref in ((w1_ref, b1_ref), (w2_ref, b2_ref), (w3_ref, b3_ref)):
            hdn = jnp.sin(fr * (jnp.dot(w_ref[...], hdn, precision=_HI, preferred_element_type=F32) + b_ref[...]))
        hdn = hdn.astype(BF16)
        col = col_ref[...]
        window = jnp.exp(-col[:, 2:3] * dl_ref[...])
        for o in range(HY_ORDER):
            base = o * 2 * HY_WIDTH
            k = None
            for d in ((0, 1) if direction is None else (direction,)):
                kd = lax.dot_general(hdn, wo_ref[:, base + d * HY_WIDTH:base + (d + 1) * HY_WIDTH], _TN,
                                     preferred_element_type=F32)
                if direction is None:
                    kd = kd * col[:, d:d + 1]
                k = kd if k is None else k + kd
            o_ref[o] = (k * window).astype(o_ref.dtype)

    pl.when(i < n_fwd)(lambda: taps(0))
    pl.when(i >= first_bwd)(lambda: taps(1))
    pl.when((i >= n_fwd) & (i < first_bwd))(lambda: taps(None))


def hyena_filter(T, w1, b1, w2, b2, w3, b3, freq, wout):
    tab, col, deltas = _filter_tables(T)
    n = tab.shape[1]
    hid = w1.shape[1]
    w1t = jnp.zeros((hid, FILT_COLS), F32).at[:, :w1.shape[0]].set(w1.T)
    wout = wout.astype(BF16)
    full = lambda i: (0, 0)
    vec = lambda a: a.reshape(-1, 1)
    n_fwd = T // FILT_ROWS
    first_bwd = -(-(n - T + 1) // FILT_ROWS)
    return pl.pallas_call(
        functools.partial(_filter_kernel, n_fwd=n_fwd, first_bwd=first_bwd),
        grid=(n // FILT_ROWS,),
        in_specs=[
            pl.BlockSpec((FILT_COLS, FILT_ROWS), lambda i: (0, i)),
            pl.BlockSpec((FILT_ROWS, col.shape[1]), lambda i: (i, 0)),
            pl.BlockSpec(w1t.shape, full), pl.BlockSpec((hid, 1), full),
            pl.BlockSpec(w2.shape, full), pl.BlockSpec((hid, 1), full),
            pl.BlockSpec(w3.shape, full), pl.BlockSpec((hid, 1), full),
            pl.BlockSpec((hid, 1), full),
            pl.BlockSpec(wout.shape, full),
            pl.BlockSpec((1, HY_WIDTH), full),
        ],
        out_specs=pl.BlockSpec((HY_ORDER, FILT_ROWS, HY_WIDTH), lambda i: (0, i, 0)),
        out_shape=jax.ShapeDtypeStruct((HY_ORDER, n, HY_WIDTH), BF16),
        compiler_params=pltpu.CompilerParams(
            dimension_semantics=("parallel",), vmem_limit_bytes=VMEM_LIMIT),
        name="hyena_filter",
    )(tab, col, w1t, vec(b1), w2.T, vec(b2), w3.T, vec(b3), vec(freq), wout, deltas)


SUB = 8
COLS = 16
LANES = 128


def _columns_matmul(x_ref, m, xs, ys):
    rows_in = x_ref.shape[0] * x_ref.shape[1]
    rows_out = m.shape[0]
    x = x_ref[...].astype(F32).reshape(rows_in, COLS, LANES)
    for half in range(COLS // SUB):
        xs[half] = x[:, half * SUB:(half + 1) * SUB, :].reshape(rows_in * SUB, LANES)
        for j in range(SUB):
            xj = xs[half, pl.ds(j, rows_in, stride=SUB), :].astype(BF16)
            ys[half, pl.ds(j, rows_out, stride=SUB), :] = jnp.dot(m, xj, preferred_element_type=F32)
    return jnp.concatenate([ys[half].reshape(rows_out, SUB, LANES) for half in range(COLS // SUB)], axis=1)


def _dft_rows_kernel(x_ref, m_ref, o_ref, xs, ys):
    y = _columns_matmul(x_ref, m_ref[...], xs, ys)
    o_ref[...] = y.reshape(o_ref.shape).astype(o_ref.dtype)


def dft_stage1(x, m, col_block, n1):
    G, nb, r, n2, _ = x.shape
    C = HY_WIDTH
    ct = C // LANES
    return pl.pallas_call(
        _dft_rows_kernel,
        grid=(G, n2 // COLS, ct),
        in_specs=[
            pl.BlockSpec((None, nb, r, COLS, LANES), lambda g, j, c: (g, 0, 0, j, col_block * ct + c)),
            pl.BlockSpec(m.shape, lambda g, j, c: (0, 0)),
        ],
        out_specs=pl.BlockSpec((None, 2, n1, COLS, LANES), lambda g, j, c: (g, 0, 0, j, c)),
        out_shape=jax.ShapeDtypeStruct((G, 2, n1, n2, C), BF16),
        scratch_shapes=[pltpu.VMEM((COLS // SUB, nb * r * SUB, LANES), F32),
                        pltpu.VMEM((COLS // SUB, 2 * n1 * SUB, LANES), F32)],
        compiler_params=pltpu.CompilerParams(
            dimension_semantics=("parallel", "parallel", "parallel"), vmem_limit_bytes=VMEM_LIMIT),
        name="dft_stage1",
    )(x, m)


def _filter_spec_kernel(y_ref, g_ref, o_ref):
    y = y_ref[...].reshape(2 * FFT_N2, -1)
    o_ref[...] = jnp.dot(g_ref[...], y, preferred_element_type=F32).reshape(o_ref.shape).astype(o_ref.dtype)


def filter_spectrum(yf, gblk):
    G, _, n1, n2, C = yf.shape
    blk = pl.BlockSpec((None, 2, None, n2, C), lambda g, k: (g, 0, k, 0, 0))
    return pl.pallas_call(
        _filter_spec_kernel,
        grid=(G, n1),
        in_specs=[blk, pl.BlockSpec((None, 2 * n2, 2 * n2), lambda g, k: (k, 0, 0))],
        out_specs=blk,
        out_shape=jax.ShapeDtypeStruct(yf.shape, BF16),
        compiler_params=pltpu.CompilerParams(
            dimension_semantics=("parallel", "parallel"), vmem_limit_bytes=VMEM_LIMIT),
        name="filter_spectrum",
    )(yf, gblk)


def _spectral_kernel(y_ref, kf_ref, g_ref, gi_ref, o_ref):
    y = y_ref[...].reshape(2 * FFT_N2, -1)
    z = jnp.dot(g_ref[...], y, preferred_element_type=F32)
    zr, zi = z[:FFT_N2], z[FFT_N2:]
    kr, ki = kf_ref[0].astype(F32), kf_ref[1].astype(F32)
    w = jnp.concatenate([zr * kr - zi * ki, zr * ki + zi * kr], axis=0).astype(BF16)
    v = jnp.dot(gi_ref[...], w, preferred_element_type=F32)
    o_ref[...] = v.reshape(o_ref.shape).astype(o_ref.dtype)


def spectral_stage(y, kf, order, gblk, giblk):
    _, n1, n2, C = y.shape
    blk = pl.BlockSpec((2, None, n2, C), lambda k: (0, k, 0, 0))
    mat = pl.BlockSpec((None, 2 * n2, 2 * n2), lambda k: (k, 0, 0))
    return pl.pallas_call(
        _spectral_kernel,
        grid=(n1,),
        in_specs=[blk, pl.BlockSpec((None, 2, None, n2, C), lambda k: (order, 0, k, 0, 0)), mat, mat],
        out_specs=blk,
        out_shape=jax.ShapeDtypeStruct(y.shape, BF16),
        compiler_params=pltpu.CompilerParams(
            dimension_semantics=("parallel",), vmem_limit_bytes=VMEM_LIMIT),
        name="hyena_spectral",
    )(y, kf, gblk, giblk)


def _idft_gate_kernel(v_ref, m_ref, gate_ref, z_ref, skip_ref, o_ref, xs, ys):
    conv = _columns_matmul(v_ref, m_ref[...], xs, ys).reshape(o_ref.shape)
    o_ref[...] = gate_ref[...] * (conv + z_ref[...] * skip_ref[...])


def idft_gate(v, m1inv, uc5, gate_block, z5, z_block, skip):
    _, n1, n2, C = v.shape
    B, r = uc5.shape[0], uc5.shape[1]
    ct = C // LANES
    rows = lambda blk: pl.BlockSpec((B, r, COLS, LANES), lambda j, c: (0, 0, j, blk * ct + c))
    return pl.pallas_call(
        _idft_gate_kernel,
        grid=(n2 // COLS, ct),
        in_specs=[
            pl.BlockSpec((2, n1, COLS, LANES), lambda j, c: (0, 0, j, c)),
            pl.BlockSpec(m1inv.shape, lambda j, c: (0, 0)),
            rows(gate_block), rows(z_block),
            pl.BlockSpec((1, LANES), lambda j, c: (0, c)),
        ],
        out_specs=rows(0),
        out_shape=jax.ShapeDtypeStruct((B, r, n2, C), F32),
        scratch_shapes=[pltpu.VMEM((COLS // SUB, 2 * n1 * SUB, LANES), F32),
                        pltpu.VMEM((COLS // SUB, B * r * SUB, LANES), F32)],
        compiler_params=pltpu.CompilerParams(
            dimension_semantics=("parallel", "parallel"), vmem_limit_bytes=VMEM_LIMIT),
        name="hyena_idft_gate",
    )(v, m1inv, uc5, z5, skip.reshape(1, C))


def hyena_mixer(p, conv_w, conv_b, fw1, fb1, fw2, fb2, fw3, fb3, freq, wout, skip):
    B, T, _ = p.shape
    assert B == 2, "the two batch rows are packed as one complex sequence"
    r, n1 = _fft_dims(T)
    n2 = FFT_N2
    Tp = r * n2
    m1, m1f, m1inv, gblk, giblk = _fft_tables(T)
    kc = hyena_filter(T, fw1, fb1, fw2, fb2, fw3, fb3, freq, wout)
    yf = dft_stage1(kc.reshape(HY_ORDER, 1, n1, n2, HY_WIDTH), m1f, 0, n1)
    kf = filter_spectrum(yf, gblk)
    uc = hyena_short_conv(p, conv_w, conv_b, Tp)
    uc4 = uc.reshape(B, r, n2, (HY_ORDER + 1) * HY_WIDTH)
    z4, zblk = uc4, HY_ORDER
    for o in range(HY_ORDER):
        y = dft_stage1(z4[None], m1, zblk, n1)[0]
        v = spectral_stage(y, kf, o, gblk, giblk)
        z4, zblk = idft_gate(v, m1inv, uc4, o, z4, zblk, skip[o]), 0
    return z4.reshape(B, Tp, HY_WIDTH)


def kernel(x, meta_tokens, norm1_g, w_in, hy_conv_w, hy_conv_b, hy_f_w1, hy_f_b1, hy_f_w2, hy_f_b2, hy_f_w3, hy_f_b3, hy_f_freq, hy_f_wout, hy_skip, lru_conv_w, lru_conv_b, lru_gate_w, lru_gate_b, lru_lambda, w_branch, w_out, norm2_g, ffn_w_gu, ffn_w_down, final_g):
    B = x.shape[0]
    meta = jnp.broadcast_to(meta_tokens[None].astype(x.dtype), (B, N_META, D_MODEL))
    h = jnp.concatenate([meta, x], axis=1)
    T = h.shape[1]
    for l in range(DEPTH):
        p = norm_proj(h.reshape(B * T, D_MODEL), norm1_g[l], w_in[l].astype(BF16)).reshape(B, T, D_IN)
        y_hy = hyena_mixer(p, hy_conv_w[l], hy_conv_b[l], hy_f_w1[l], hy_f_b1[l], hy_f_w2[l], hy_f_b2[l],
                           hy_f_w3[l], hy_f_b3[l], hy_f_freq[l], hy_f_wout[l], hy_skip[l])
        y_lru = rglru_mixer(p, lru_conv_w[l], lru_conv_b[l], lru_gate_w[l], lru_gate_b[l], lru_lambda[l])
        y_ret = retention_mixer(p)
        h = merge_out(h, y_hy, y_lru, y_ret, p, w_branch[l].astype(BF16), w_out[l].astype(BF16))
        h = ffn(h.reshape(B * T, D_MODEL), norm2_g[l], ffn_w_gu[l].astype(BF16), ffn_w_down[l].astype(BF16),
                final_g, final_norm=(l == DEPTH - 1)).reshape(B, T, D_MODEL)
    return h[:, N_META:]
```

```python
import functools
import math

import jax
import jax.numpy as jnp
import numpy as np
from jax import lax
from jax.experimental import pallas as pl
from jax.experimental.pallas import tpu as pltpu

F32 = jnp.float32
BF16 = jnp.bfloat16

D_MODEL = 1024
N_META = 16
EPS = 1e-6
DEPTH = 2

HY_WIDTH = D_MODEL
HY_ORDER = 2
HY_SHORT = 3
HY_BANDS = 16
HY_DECAY_PCT_SHORT = 0.3
HY_DECAY_PCT_LONG = 1.5
HY_DECAY_TARGET = 1e-2

LRU_WIDTH = D_MODEL
LRU_BLOCKS = 8
LRU_BLOCK = LRU_WIDTH // LRU_BLOCKS
LRU_CONV = 4
LRU_C = 8.0

RET_HEADS = 4
RET_DK = D_MODEL // RET_HEADS
RET_DV = D_MODEL // RET_HEADS
RET_WIDTH = RET_HEADS * RET_DV
RET_CHUNK = 128
ROPE_BASE = 10000.0

N_BRANCH = 3
D_FF = ((8 * D_MODEL + 3 * 256 - 1) // (3 * 256)) * 256

OFF_LRU_X = HY_ORDER * HY_WIDTH + HY_WIDTH
OFF_LRU_G = OFF_LRU_X + LRU_WIDTH
OFF_Q = OFF_LRU_G + LRU_WIDTH
OFF_K = OFF_Q + RET_HEADS * RET_DK
OFF_V = OFF_K + RET_HEADS * RET_DK
OFF_G = OFF_V + RET_WIDTH
OFF_MERGE = OFF_G + RET_WIDTH
D_IN = OFF_MERGE + N_BRANCH * D_MODEL

ROW_TILE = 912
MERGE_ROW_TILE = 432
VMEM_LIMIT = 56 * 1024 * 1024


def _rms(x, g):
    ms = jnp.mean(x * x, axis=-1, keepdims=True)
    return x * lax.rsqrt(ms + EPS) * g


def _sigmoid(x):
    return 0.5 * jnp.tanh(0.5 * x) + 0.5


def _norm_proj_kernel(h_ref, g_ref, w_ref, o_ref, u_scr):
    @pl.when(pl.program_id(1) == 0)
    def _():
        u_scr[...] = _rms(h_ref[...], g_ref[...]).astype(BF16)

    o_ref[...] = jnp.dot(u_scr[...], w_ref[...], preferred_element_type=F32).astype(o_ref.dtype)


def norm_proj(h2, g, w_bf16, tn=2048):
    m, d = h2.shape
    n = w_bf16.shape[1]
    return pl.pallas_call(
        _norm_proj_kernel,
        grid=(m // ROW_TILE, n // tn),
        in_specs=[
            pl.BlockSpec((ROW_TILE, d), lambda i, j: (i, 0)),
            pl.BlockSpec((1, d), lambda i, j: (0, 0)),
            pl.BlockSpec((d, tn), lambda i, j: (0, j)),
        ],
        out_specs=pl.BlockSpec((ROW_TILE, tn), lambda i, j: (i, j)),
        out_shape=jax.ShapeDtypeStruct((m, n), BF16),
        scratch_shapes=[pltpu.VMEM((ROW_TILE, d), BF16)],
        compiler_params=pltpu.CompilerParams(
            dimension_semantics=("parallel", "arbitrary"), vmem_limit_bytes=VMEM_LIMIT),
        name="norm_proj",
    )(h2, g.reshape(1, d), w_bf16)


def _merge_kernel(h_ref, yh_ref, yl_ref, yr_ref, gate_ref, wb_ref, wo_ref, o_ref):
    merged = None
    for bi, y_ref in enumerate((yh_ref, yl_ref, yr_ref)):
        proj = jnp.dot(y_ref[...].astype(BF16), wb_ref[bi], preferred_element_type=F32)
        term = _sigmoid(gate_ref[:, bi * D_MODEL:(bi + 1) * D_MODEL].astype(F32)) * proj
        merged = term if merged is None else merged + term
    o_ref[...] = h_ref[...] + jnp.dot(merged.astype(BF16), wo_ref[...], preferred_element_type=F32)


def merge_out(h, y_hy, y_lru, y_ret, p, wb_bf16, wo_bf16):
    B, T, d = h.shape
    tm = MERGE_ROW_TILE
    assert T % tm == 0
    row = lambda b, i: (b, i, 0)
    gate_blk = OFF_MERGE // (N_BRANCH * D_MODEL)
    blk = pl.BlockSpec((None, tm, d), row)
    return pl.pallas_call(
        _merge_kernel,
        grid=(B, T // tm),
        in_specs=[
            blk, blk, blk, blk,
            pl.BlockSpec((None, tm, N_BRANCH * D_MODEL), lambda b, i: (b, i, gate_blk)),
            pl.BlockSpec((N_BRANCH, d, d), lambda b, i: (0, 0, 0), pipeline_mode=pl.Buffered(1)),
            pl.BlockSpec((d, d), lambda b, i: (0, 0), pipeline_mode=pl.Buffered(1)),
        ],
        out_specs=blk,
        out_shape=jax.ShapeDtypeStruct((B, T, d), F32),
        compiler_params=pltpu.CompilerParams(
            dimension_semantics=("parallel", "parallel"), vmem_limit_bytes=VMEM_LIMIT),
        name="merge_out",
    )(h, y_hy, y_lru, y_ret, p, wb_bf16, wo_bf16)


FF_CHUNK = 256


def _ffn_kernel(h_ref, g_ref, wgu_ref, wd_ref, fg_ref, o_ref, v_scr, acc_scr, *, final_norm):
    h = h_ref[...]
    v_scr[...] = _rms(h, g_ref[...]).astype(BF16)
    acc_scr[...] = h

    def body(c, carry):
        off = pl.multiple_of(c * FF_CHUNK, FF_CHUNK)
        v = v_scr[...]
        gate = jnp.dot(v, wgu_ref[:, pl.ds(off, FF_CHUNK)], preferred_element_type=F32)
        up = jnp.dot(v, wgu_ref[:, pl.ds(D_FF + off, FF_CHUNK)], preferred_element_type=F32)
        hid = (gate * _sigmoid(gate) * up).astype(BF16)
        acc_scr[...] += jnp.dot(hid, wd_ref[pl.ds(off, FF_CHUNK), :], preferred_element_type=F32)
        return carry

    lax.fori_loop(0, D_FF // FF_CHUNK, body, 0)
    out = acc_scr[...]
    if final_norm:
        out = _rms(out, fg_ref[...])
    o_ref[...] = out


def ffn(h2, g, wgu_bf16, wd_bf16, final_g, final_norm):
    m, d = h2.shape
    row = lambda i: (i, 0)
    return pl.pallas_call(
        functools.partial(_ffn_kernel, final_norm=final_norm),
        grid=(m // ROW_TILE,),
        in_specs=[
            pl.BlockSpec((ROW_TILE, d), row),
            pl.BlockSpec((1, d), lambda i: (0, 0)),
            pl.BlockSpec((d, 2 * D_FF), lambda i: (0, 0), pipeline_mode=pl.Buffered(1)),
            pl.BlockSpec((D_FF, d), lambda i: (0, 0), pipeline_mode=pl.Buffered(1)),
            pl.BlockSpec((1, d), lambda i: (0, 0)),
        ],
        out_specs=pl.BlockSpec((ROW_TILE, d), row),
        out_shape=jax.ShapeDtypeStruct((m, d), F32),
        scratch_shapes=[pltpu.VMEM((ROW_TILE, d), BF16), pltpu.VMEM((ROW_TILE, d), F32)],
        compiler_params=pltpu.CompilerParams(
            dimension_semantics=("parallel",), vmem_limit_bytes=VMEM_LIMIT),
        name="ffn",
    )(h2, g.reshape(1, d), wgu_bf16, wd_bf16, final_g.reshape(1, d))


LRU_ROWS = 216
COPY_ROWS = 432
SCAN_UNROLL = 6
HALO = 8


def _lru_kernel(x_ref, g_ref, cw_ref, cb_ref, gw_ref, gb_ref, lam_ref, o_ref,
                xs, gs, a0, b0, a1, b1, *, T):
    W = LRU_BLOCK
    m = T // 8
    zeros8 = jnp.zeros((HALO, W), F32)
    xs[pl.ds(0, HALO), :] = zeros8
    xs[pl.ds(HALO + T, HALO), :] = zeros8

    def copy_body(c, carry):
        r0 = pl.multiple_of(c * COPY_ROWS, 16)
        xs[pl.ds(HALO + r0, COPY_ROWS), :] = x_ref[pl.ds(r0, COPY_ROWS), :].astype(F32)
        gs[pl.ds(r0, COPY_ROWS), :] = g_ref[pl.ds(r0, COPY_ROWS), :].astype(F32)
        return carry

    lax.fori_loop(0, T // COPY_ROWS, copy_body, 0)

    cw = cw_ref[...]
    cb = cb_ref[...]
    gb = gb_ref[...]
    gw = gw_ref[...]
    nl = -lam_ref[...]
    sp = jnp.maximum(nl, 0.0) + jnp.log1p(jnp.exp(-jnp.abs(nl)))

    def gate_body(c, carry):
        r0 = pl.multiple_of(c * LRU_ROWS, 8)
        xc = cb
        for k in range(LRU_CONV):
            xc = xc + cw[k:k + 1] * xs[pl.ds(r0 + (HALO - LRU_CONV // 2 + k), LRU_ROWS), :]
        gates = _sigmoid(jnp.dot(xc.astype(BF16), gw, preferred_element_type=F32) + gb)
        for d, (a_s, b_s) in enumerate(((a0, b0), (a1, b1))):
            r = gates[:, (2 * d) * W:(2 * d + 1) * W]
            i = gates[:, (2 * d + 1) * W:(2 * d + 2) * W]
            log_a = (-LRU_C * sp[d:d + 1]) * r
            a = jnp.exp(log_a)
            a_s[pl.ds(r0, LRU_ROWS), :] = a
            b_s[pl.ds(r0, LRU_ROWS), :] = jnp.sqrt(1.0 - a * a) * i * xc
        return carry

    lax.fori_loop(0, T // LRU_ROWS, gate_body, 0, unroll=2)

    def scan_body(j, carry):
        hf, pf, hb, pb = carry
        rf = pl.ds(j, 8, stride=m)
        rb = pl.ds(m - 1 - j, 8, stride=m)
        af = a0[rf, :]
        hf = af * hf + b0[rf, :]
        pf = af * pf
        b0[rf, :] = hf
        a0[rf, :] = pf
        ab = a1[rb, :]
        hb = ab * hb + b1[rb, :]
        pb = ab * pb
        b1[rb, :] = hb
        a1[rb, :] = pb
        return hf, pf, hb, pb

    z = jnp.zeros((8, W), F32)
    o = jnp.ones((8, W), F32)
    hf, pf, hb, pb = lax.fori_loop(0, m, scan_body, (z, o, z, o), unroll=2)

    row = lax.broadcasted_iota(jnp.int32, (8, W), 0)
    cf = z
    cbk = z
    for _ in range(7):
        cf = jnp.where(row == 0, 0.0, pltpu.roll(hf + pf * cf, 1, axis=0))
        cbk = jnp.where(row == 7, 0.0, pltpu.roll(hb + pb * cbk, 7, axis=0))

    def out_body(j, carry):
        rows = pl.ds(j, 8, stride=m)
        h = b0[rows, :] + a0[rows, :] * cf + b1[rows, :] + a1[rows, :] * cbk
        o_ref[rows, :] = h * jax.nn.gelu(gs[rows, :])
        return carry

    lax.fori_loop(0, m, out_body, 0, unroll=SCAN_UNROLL)


def rglru_mixer(p, conv_w, conv_b, gate_w, gate_b, lam):
    B, T, _ = p.shape
    W = LRU_BLOCK
    assert T % LRU_ROWS == 0 and T % COPY_ROWS == 0
    gw = jnp.transpose(gate_w, (2, 3, 0, 1, 4)).reshape(LRU_BLOCKS, W, 4 * W).astype(BF16)
    gb = jnp.transpose(gate_b.reshape(2, 2, LRU_BLOCKS, W), (2, 0, 1, 3)).reshape(LRU_BLOCKS, 1, 4 * W)
    xoff = OFF_LRU_X // W
    goff = OFF_LRU_G // W
    seq = pltpu.VMEM((T, W), F32)
    return pl.pallas_call(
        functools.partial(_lru_kernel, T=T),
        grid=(B, LRU_BLOCKS),
        in_specs=[
            pl.BlockSpec((None, T, W), lambda b, n: (b, 0, xoff + n)),
            pl.BlockSpec((None, T, W), lambda b, n: (b, 0, goff + n)),
            pl.BlockSpec((LRU_CONV, W), lambda b, n: (0, n)),
            pl.BlockSpec((1, W), lambda b, n: (0, n)),
            pl.BlockSpec((None, W, 4 * W), lambda b, n: (n, 0, 0)),
            pl.BlockSpec((None, 1, 4 * W), lambda b, n: (n, 0, 0)),
            pl.BlockSpec((2, W), lambda b, n: (0, n)),
        ],
        out_specs=pl.BlockSpec((None, T, W), lambda b, n: (b, 0, n)),
        out_shape=jax.ShapeDtypeStruct((B, T, LRU_WIDTH), F32),
        scratch_shapes=[pltpu.VMEM((T + 2 * HALO, W), F32), seq, seq, seq, seq, seq],
        compiler_params=pltpu.CompilerParams(
            dimension_semantics=("parallel", "parallel"), vmem_limit_bytes=VMEM_LIMIT),
        name="rglru",
    )(p, p, conv_w, conv_b.reshape(1, -1), gw, gb, lam)


RET_ROWS = 432


def _ret_tables(T):
    C = RET_ROWS
    half = RET_DK // 2
    inv = ROPE_BASE ** (-np.linspace(0.0, 1.0, half, dtype=np.float32))
    ang = np.arange(T, dtype=np.float32)[:, None] * inv[None, :].astype(np.float32)
    cos = np.cos(ang.astype(np.float64)).astype(np.float32)
    sin = np.sin(ang.astype(np.float64)).astype(np.float32)
    log_gamma = np.log1p(-(2.0 ** (-5.0 - np.arange(RET_HEADS, dtype=np.float64))))
    idx = np.arange(C, dtype=np.float64)
    dmat = np.exp(np.abs(idx[:, None] - idx[None, :])[None] * log_gamma[:, None, None])
    sc = np.stack([np.exp((idx + 1.0)[:, None] * log_gamma[None]),
                   np.exp((C - 1.0 - idx)[:, None] * log_gamma[None]),
                   np.exp((C - idx)[:, None] * log_gamma[None]),
                   np.exp(idx[:, None] * log_gamma[None])])
    chunk_decay = np.exp(C * log_gamma)
    return (jnp.asarray(cos), jnp.asarray(sin), jnp.asarray(dmat, F32),
            jnp.asarray(sc, F32), [float(c) for c in chunk_decay])


def _rot(x, cos, sin):
    half = RET_DK // 2
    x1, x2 = x[:, :half], x[:, half:]
    return jnp.concatenate([x1 * cos - x2 * sin, x1 * sin + x2 * cos], axis=-1)


_TN = (((0,), (0,)), ((), ()))
_NT = (((1,), (1,)), ((), ()))


def _ret_fwd_kernel(q_ref, k_ref, v_ref, cos_ref, sin_ref, d_ref, sc_ref, o_ref, s_scr, *, decay):
    @pl.when(pl.program_id(1) == 0)
    def _():
        s_scr[...] = jnp.zeros_like(s_scr)

    cos = cos_ref[...]
    sin = sin_ref[...]
    for h in range(RET_HEADS):
        cols = slice(h * RET_DK, (h + 1) * RET_DK)
        q = _rot(q_ref[:, cols].astype(F32), cos, sin)
        k = _rot(k_ref[:, cols].astype(F32), cos, sin) * (RET_DK ** -0.5)
        v = v_ref[:, cols]
        scores = lax.dot_general(q.astype(BF16), k.astype(BF16), _NT, preferred_element_type=F32)
        scores = scores * d_ref[h]
        y = jnp.dot(scores.astype(BF16), v, preferred_element_type=F32)
        state = s_scr[h]
        qx = (q * sc_ref[0, :, h:h + 1]).astype(BF16)
        y = y + jnp.dot(qx, state.astype(BF16), preferred_element_type=F32)
        o_ref[:, cols] = y
        kz = (k * sc_ref[1, :, h:h + 1]).astype(BF16)
        s_scr[h] = decay[h] * state + lax.dot_general(kz, v, _TN, preferred_element_type=F32)


def _ret_bwd_kernel(q_ref, k_ref, v_ref, g_ref, y_ref, cos_ref, sin_ref, sc_ref, o_ref, s_scr, *, decay):
    @pl.when(pl.program_id(1) == 0)
    def _():
        s_scr[...] = jnp.zeros_like(s_scr)

    cos = cos_ref[...]
    sin = sin_ref[...]
    for h in range(RET_HEADS):
        cols = slice(h * RET_DK, (h + 1) * RET_DK)
        q = _rot(q_ref[:, cols].astype(F32), cos, sin)
        k = _rot(k_ref[:, cols].astype(F32), cos, sin) * (RET_DK ** -0.5)
        v = v_ref[:, cols]
        state = s_scr[h]
        qx = (q * sc_ref[2, :, h:h + 1]).astype(BF16)
        y = y_ref[:, cols] + jnp.dot(qx, state.astype(BF16), preferred_element_type=F32)
        kz = (k * sc_ref[3, :, h:h + 1]).astype(BF16)
        s_scr[h] = decay[h] * state + lax.dot_general(kz, v, _TN, preferred_element_type=F32)
        mu = jnp.mean(y, axis=-1, keepdims=True)
        yc = y - mu
        var = jnp.mean(yc * yc, axis=-1, keepdims=True)
        g = g_ref[:, cols].astype(F32)
        o_ref[:, cols] = g * _sigmoid(g) * (yc * lax.rsqrt(var + EPS))


def retention_mixer(p):
    B, T, _ = p.shape
    C = RET_ROWS
    assert T % C == 0
    n = T // C
    cos, sin, dmat, sc, decay = _ret_tables(T)
    half = RET_DK // 2
    qb, kb, vb, gb = (o // RET_WIDTH for o in (OFF_Q, OFF_K, OFF_V, OFF_G))
    state = pltpu.VMEM((RET_HEADS, RET_DK, RET_DV), F32)
    params = pltpu.CompilerParams(dimension_semantics=("parallel", "arbitrary"), vmem_limit_bytes=VMEM_LIMIT)

    def col(blk, rev):
        if rev:
            return pl.BlockSpec((None, C, RET_WIDTH), lambda b, i: (b, n - 1 - i, blk))
        return pl.BlockSpec((None, C, RET_WIDTH), lambda b, i: (b, i, blk))

    def tab(rev):
        if rev:
            return pl.BlockSpec((C, half), lambda b, i: (n - 1 - i, 0))
        return pl.BlockSpec((C, half), lambda b, i: (i, 0))

    const3 = lambda b, i: (0, 0, 0)
    y1 = pl.pallas_call(
        functools.partial(_ret_fwd_kernel, decay=decay),
        grid=(B, n),
        in_specs=[col(qb, False), col(kb, False), col(vb, False), tab(False), tab(False),
                  pl.BlockSpec((RET_HEADS, C, C), const3), pl.BlockSpec((4, C, RET_HEADS), const3)],
        out_specs=col(0, False),
        out_shape=jax.ShapeDtypeStruct((B, T, RET_WIDTH), F32),
        scratch_shapes=[state],
        compiler_params=params,
        name="ret_fwd",
    )(p, p, p, cos, sin, dmat, sc)
    return pl.pallas_call(
        functools.partial(_ret_bwd_kernel, decay=decay),
        grid=(B, n),
        in_specs=[col(qb, True), col(kb, True), col(vb, True), col(gb, True), col(0, True),
                  tab(True), tab(True), pl.BlockSpec((4, C, RET_HEADS), const3)],
        out_specs=col(0, True),
        out_shape=jax.ShapeDtypeStruct((B, T, RET_WIDTH), F32),
        scratch_shapes=[state],
        compiler_params=params,
        name="ret_bwd",
    )(p, p, p, p, y1, cos, sin, sc)


FFT_N2 = 128
HY_ROWS = 216


def _fft_dims(T):
    r = -(-T // FFT_N2)
    r = -(-r // 8) * 8
    return r, 2 * r


def _cblock(z):
    return np.block([[z.real, -z.imag], [z.imag, z.real]])


def _fft_tables(T):
    r, n1 = _fft_dims(T)
    n2 = FFT_N2
    n = n1 * n2
    a1 = np.arange(n1, dtype=np.float64)
    a2 = np.arange(n2, dtype=np.float64)
    f1 = np.exp(-2j * np.pi * np.outer(a1, a1) / n1)
    m1 = _cblock(f1[:, :r])
    m1f = np.concatenate([f1.real, f1.imag], axis=0)
    m1inv = _cblock(np.conj(f1.T)[:r, :] / n)
    f2 = np.exp(-2j * np.pi * np.outer(a2, a2) / n2)
    tw = np.exp(-2j * np.pi * np.outer(a1, a2) / n)
    g = f2[None, :, :] * tw[:, None, :]
    gi = np.conj(np.transpose(g, (0, 2, 1)))
    gblk = np.stack([_cblock(g[k]) for k in range(n1)])
    giblk = np.stack([_cblock(gi[k]) for k in range(n1)])
    cast = lambda a: jnp.asarray(a, F32).astype(BF16)
    return cast(m1), cast(m1f), cast(m1inv), cast(gblk), cast(giblk)


def _short_conv_kernel(x_ref, w_ref, b_ref, o_ref, xs, *, T, Tp):
    W = x_ref.shape[-1]
    zeros8 = jnp.zeros((HALO, W), F32)
    xs[pl.ds(0, HALO), :] = zeros8
    xs[pl.ds(HALO + T, HALO), :] = zeros8

    def copy_body(c, carry):
        r0 = pl.multiple_of(c * COPY_ROWS, 16)
        xs[pl.ds(HALO + r0, COPY_ROWS), :] = x_ref[pl.ds(r0, COPY_ROWS), :].astype(F32)
        return carry

    lax.fori_loop(0, T // COPY_ROWS, copy_body, 0)
    w = w_ref[...]
    b = b_ref[...]

    def conv_body(c, carry):
        r0 = pl.multiple_of(c * HY_ROWS, 8)
        win = xs[pl.ds(r0, HY_ROWS + 2 * HALO), :]
        y = b + w[0:1] * win[HALO - 1:HALO - 1 + HY_ROWS]
        y = y + w[1:2] * win[HALO:HALO + HY_ROWS]
        y = y + w[2:3] * win[HALO + 1:HALO + 1 + HY_ROWS]
        o_ref[pl.ds(r0, HY_ROWS), :] = y
        return carry

    lax.fori_loop(0, T // HY_ROWS, conv_body, 0)
    o_ref[pl.ds(T, Tp - T), :] = jnp.zeros((Tp - T, W), F32)


def hyena_short_conv(p, conv_w, conv_b, Tp):
    B, T, _ = p.shape
    W = 128
    ncol = (HY_ORDER + 1) * HY_WIDTH
    return pl.pallas_call(
        functools.partial(_short_conv_kernel, T=T, Tp=Tp),
        grid=(B, ncol // W),
        in_specs=[
            pl.BlockSpec((None, T, W), lambda b, n: (b, 0, n)),
            pl.BlockSpec((HY_SHORT, W), lambda b, n: (0, n)),
            pl.BlockSpec((1, W), lambda b, n: (0, n)),
        ],
        out_specs=pl.BlockSpec((None, Tp, W), lambda b, n: (b, 0, n)),
        out_shape=jax.ShapeDtypeStruct((B, Tp, ncol), F32),
        scratch_shapes=[pltpu.VMEM((T + 2 * HALO, W), F32)],
        compiler_params=pltpu.CompilerParams(
            dimension_semantics=("parallel", "parallel"), vmem_limit_bytes=VMEM_LIMIT),
        name="hyena_short_conv",
    )(p, conv_w, conv_b.reshape(1, -1))


def _filter_tables(T):
    r, n1 = _fft_dims(T)
    n = n1 * FFT_N2
    idx = np.arange(n)
    fwd = idx < T
    bwd = idx > n - T
    lag = np.where(fwd, idx, np.where(bwd, n - idx, 0)).astype(np.float32)
    t_norm = lag / np.float32(max(T - 1, 1))
    bands = np.linspace(1e-4, HY_BANDS - 1, HY_BANDS, dtype=np.float32)
    ang = (np.float32(2.0 * math.pi / T) * lag[:, None]) * bands[None, :]
    feat = np.concatenate([t_norm[:, None], np.cos(ang), np.sin(ang)], axis=-1)
    valid = (fwd | bwd)[:, None]
    tab = np.zeros((FILT_COLS, n), np.float32)
    tab[:feat.shape[1]] = np.where(valid, feat, 0.0).T
    col = np.stack([fwd, bwd, t_norm], axis=1).astype(np.float32)
    max_decay = math.log(HY_DECAY_TARGET) / HY_DECAY_PCT_SHORT
    min_decay = math.log(HY_DECAY_TARGET) / HY_DECAY_PCT_LONG
    deltas = np.abs(np.linspace(min_decay, max_decay, HY_WIDTH, dtype=np.float32))
    return jnp.asarray(tab), jnp.asarray(col), jnp.asarray(deltas).reshape(1, -1)


FILT_COLS = 40
FILT_ROWS = 512
_HI = lax.Precision.HIGHEST


def _filter_kernel(tab_ref, col_ref, w1_ref, b1_ref, w2_ref, b2_ref, w3_ref, b3_ref, fr_ref, wo_ref, dl_ref,
                   o_ref, *, n_fwd, first_bwd):
    i = pl.program_id(0)

    def taps(direction):
        fr = fr_ref[...]
        hdn = tab_ref[...]
        for w_ref, b_ref in ((w1_ref, b1_ref), (w2_ref, b2_ref), (w3_ref, b3_ref)):
            hdn = jnp.sin(fr * (jnp.dot(w_ref[...], hdn, precision=_HI, preferred_element_type=F32) + b_ref[...]))
        hdn = hdn.astype(BF16)
        col = col_ref[...]
        window = jnp.exp(-col[:, 2:3] * dl_ref[...])
        for o in range(HY_ORDER):
            base = o * 2 * HY_WIDTH
            k = None
            for d in ((0, 1) if direction is None else (direction,)):
                kd = lax.dot_general(hdn, wo_ref[:, base + d * HY_WIDTH:base + (d + 1) * HY_WIDTH], _TN,
                                     preferred_element_type=F32)
                if direction is None:
                    kd = kd * col[:, d:d + 1]
                k = kd if k is None else k + kd
            o_ref[o] = (k * window).astype(o_ref.dtype)

    pl.when(i < n_fwd)(lambda: taps(0))
    pl.when(i >= first_bwd)(lambda: taps(1))
    pl.when((i >= n_fwd) & (i < first_bwd))(lambda: taps(None))


def hyena_filter(T, w1, b1, w2, b2, w3, b3, freq, wout):
    tab, col, deltas = _filter_tables(T)
    n = tab.shape[1]
    hid = w1.shape[1]
    w1t = jnp.zeros((hid, FILT_COLS), F32).at[:, :w1.shape[0]].set(w1.T)
    wout = wout.astype(BF16)
    full = lambda i: (0, 0)
    vec = lambda a: a.reshape(-1, 1)
    n_fwd = T // FILT_ROWS
    first_bwd = -(-(n - T + 1) // FILT_ROWS)
    return pl.pallas_call(
        functools.partial(_filter_kernel, n_fwd=n_fwd, first_bwd=first_bwd),
        grid=(n // FILT_ROWS,),
        in_specs=[
            pl.BlockSpec((FILT_COLS, FILT_ROWS), lambda i: (0, i)),
            pl.BlockSpec((FILT_ROWS, col.shape[1]), lambda i: (i, 0)),
            pl.BlockSpec(w1t.shape, full), pl.BlockSpec((hid, 1), full),
            pl.BlockSpec(w2.shape, full), pl.BlockSpec((hid, 1), full),
            pl.BlockSpec(w3.shape, full), pl.BlockSpec((hid, 1), full),
            pl.BlockSpec((hid, 1), full),
            pl.BlockSpec(wout.shape, full),
            pl.BlockSpec((1, HY_WIDTH), full),
        ],
        out_specs=pl.BlockSpec((HY_ORDER, FILT_ROWS, HY_WIDTH), lambda i: (0, i, 0)),
        out_shape=jax.ShapeDtypeStruct((HY_ORDER, n, HY_WIDTH), BF16),
        compiler_params=pltpu.CompilerParams(
            dimension_semantics=("parallel",), vmem_limit_bytes=VMEM_LIMIT),
        name="hyena_filter",
    )(tab, col, w1t, vec(b1), w2.T, vec(b2), w3.T, vec(b3), vec(freq), wout, deltas)


SUB = 8
COLS = 16
LANES = 128
WIDE = 256
SLABS = (WIDE // LANES) * (COLS // SUB)
K1_STEP = 4


def _columns_matmul(x_ref, m, xs, ys):
    rows_in = x_ref.shape[0] * x_ref.shape[1]
    rows_out = m.shape[0]
    lane_parts = []
    for lc in range(WIDE // LANES):
        x = x_ref[:, :, :, lc * LANES:(lc + 1) * LANES].astype(F32).reshape(rows_in, COLS, LANES)
        halves = []
        for half in range(COLS // SUB):
            s = lc * (COLS // SUB) + half
            xs[s] = x[:, half * SUB:(half + 1) * SUB, :].reshape(rows_in * SUB, LANES)
            for j in range(SUB):
                xj = xs[s, pl.ds(j, rows_in, stride=SUB), :].astype(BF16)
                ys[s, pl.ds(j, rows_out, stride=SUB), :] = jnp.dot(m, xj, preferred_element_type=F32)
            halves.append(ys[s].reshape(rows_out, SUB, LANES))
        lane_parts.append(jnp.concatenate(halves, axis=1))
    return jnp.concatenate(lane_parts, axis=2)


def _dft_rows_kernel(x_ref, m_ref, o_ref, xs, ys):
    y = _columns_matmul(x_ref, m_ref[...], xs, ys)
    o_ref[...] = y.reshape(o_ref.shape).astype(o_ref.dtype)


def dft_stage1(x, m, col_block, n1):
    G, nb, r, n2, _ = x.shape
    C = HY_WIDTH
    ct = C // WIDE
    return pl.pallas_call(
        _dft_rows_kernel,
        grid=(G, n2 // COLS, ct),
        in_specs=[
            pl.BlockSpec((None, nb, r, COLS, WIDE), lambda g, j, c: (g, 0, 0, j, col_block * ct + c)),
            pl.BlockSpec(m.shape, lambda g, j, c: (0, 0)),
        ],
        out_specs=pl.BlockSpec((None, 2, n1, COLS, WIDE), lambda g, j, c: (g, 0, 0, j, c)),
        out_shape=jax.ShapeDtypeStruct((G, 2, n1, n2, C), BF16),
        scratch_shapes=[pltpu.VMEM((SLABS, nb * r * SUB, LANES), F32),
                        pltpu.VMEM((SLABS, 2 * n1 * SUB, LANES), F32)],
        compiler_params=pltpu.CompilerParams(
            dimension_semantics=("parallel", "parallel", "parallel"), vmem_limit_bytes=VMEM_LIMIT),
        name="dft_stage1",
    )(x, m)


def _filter_spec_kernel(y_ref, g_ref, o_ref):
    for k in range(y_ref.shape[1]):
        y = y_ref[:, k].reshape(2 * FFT_N2, -1)
        z = jnp.dot(g_ref[k], y, preferred_element_type=F32)
        o_ref[:, k] = z.reshape(2, FFT_N2, -1).astype(o_ref.dtype)


def filter_spectrum(yf, gblk):
    G, _, n1, n2, C = yf.shape
    ks = K1_STEP if n1 % K1_STEP == 0 else 1
    blk = pl.BlockSpec((None, 2, ks, n2, C), lambda g, k: (g, 0, k, 0, 0))
    return pl.pallas_call(
        _filter_spec_kernel,
        grid=(G, n1 // ks),
        in_specs=[blk, pl.BlockSpec((ks, 2 * n2, 2 * n2), lambda g, k: (k, 0, 0))],
        out_specs=blk,
        out_shape=jax.ShapeDtypeStruct(yf.shape, BF16),
        compiler_params=pltpu.CompilerParams(
            dimension_semantics=("parallel", "parallel"), vmem_limit_bytes=VMEM_LIMIT),
        name="filter_spectrum",
    )(yf, gblk)


def _spectral_kernel(y_ref, kf_ref, g_ref, gi_ref, o_ref):
    for k in range(y_ref.shape[1]):
        y = y_ref[:, k].reshape(2 * FFT_N2, -1)
        z = jnp.dot(g_ref[k], y, preferred_element_type=F32)
        zr, zi = z[:FFT_N2], z[FFT_N2:]
        kr, ki = kf_ref[0, k].astype(F32), kf_ref[1, k].astype(F32)
        w = jnp.concatenate([zr * kr - zi * ki, zr * ki + zi * kr], axis=0).astype(BF16)
        v = jnp.dot(gi_ref[k], w, preferred_element_type=F32)
        o_ref[:, k] = v.reshape(2, FFT_N2, -1).astype(o_ref.dtype)


def spectral_stage(y, kf, order, gblk, giblk):
    _, n1, n2, C = y.shape
    ks = K1_STEP if n1 % K1_STEP == 0 else 1
    blk = pl.BlockSpec((2, ks, n2, C), lambda k: (0, k, 0, 0))
    mat = pl.BlockSpec((ks, 2 * n2, 2 * n2), lambda k: (k, 0, 0))
    return pl.pallas_call(
        _spectral_kernel,
        grid=(n1 // ks,),
        in_specs=[blk, pl.BlockSpec((None, 2, ks, n2, C), lambda k: (order, 0, k, 0, 0)), mat, mat],
        out_specs=blk,
        out_shape=jax.ShapeDtypeStruct(y.shape, BF16),
        compiler_params=pltpu.CompilerParams(
            dimension_semantics=("parallel",), vmem_limit_bytes=VMEM_LIMIT),
        name="hyena_spectral",
    )(y, kf, gblk, giblk)


def _idft_gate_kernel(v_ref, m_ref, gate_ref, z_ref, skip_ref, o_ref, xs, ys):
    conv = _columns_matmul(v_ref, m_ref[...], xs, ys).reshape(o_ref.shape)
    o_ref[...] = gate_ref[...] * (conv + z_ref[...] * skip_ref[...])


def idft_gate(v, m1inv, uc5, gate_block, z5, z_block, skip):
    _, n1, n2, C = v.shape
    B, r = uc5.shape[0], uc5.shape[1]
    ct = C // WIDE
    rows = lambda blk: pl.BlockSpec((B, r, COLS, WIDE), lambda j, c: (0, 0, j, blk * ct + c))
    return pl.pallas_call(
        _idft_gate_kernel,
        grid=(n2 // COLS, ct),
        in_specs=[
            pl.BlockSpec((2, n1, COLS, WIDE), lambda j, c: (0, 0, j, c)),
            pl.BlockSpec(m1inv.shape, lambda j, c: (0, 0)),
            rows(gate_block), rows(z_block),
            pl.BlockSpec((1, WIDE), lambda j, c: (0, c)),
        ],
        out_specs=rows(0),
        out_shape=jax.ShapeDtypeStruct((B, r, n2, C), F32),
        scratch_shapes=[pltpu.VMEM((SLABS, 2 * n1 * SUB, LANES), F32),
                        pltpu.VMEM((SLABS, B * r * SUB, LANES), F32)],
        compiler_params=pltpu.CompilerParams(
            dimension_semantics=("parallel", "parallel"), vmem_limit_bytes=VMEM_LIMIT),
        name="hyena_idft_gate",
    )(v, m1inv, uc5, z5, skip.reshape(1, C))


def hyena_mixer(p, conv_w, conv_b, fw1, fb1, fw2, fb2, fw3, fb3, freq, wout, skip):
    B, T, _ = p.shape
    assert B == 2, "the two batch rows are packed as one complex sequence"
    r, n1 = _fft_dims(T)
    n2 = FFT_N2
    Tp = r * n2
    m1, m1f, m1inv, gblk, giblk = _fft_tables(T)
    kc = hyena_filter(T, fw1, fb1, fw2, fb2, fw3, fb3, freq, wout)
    yf = dft_stage1(kc.reshape(HY_ORDER, 1, n1, n2, HY_WIDTH), m1f, 0, n1)
    kf = filter_spectrum(yf, gblk)
    uc = hyena_short_conv(p, conv_w, conv_b, Tp)
    uc4 = uc.reshape(B, r, n2, (HY_ORDER + 1) * HY_WIDTH)
    z4, zblk = uc4, HY_ORDER
    for o in range(HY_ORDER):
        y = dft_stage1(z4[None], m1, zblk, n1)[0]
        v = spectral_stage(y, kf, o, gblk, giblk)
        z4, zblk = idft_gate(v, m1inv, uc4, o, z4, zblk, skip[o]), 0
    return z4.reshape(B, Tp, HY_WIDTH)


def kernel(x, meta_tokens, norm1_g, w_in, hy_conv_w, hy_conv_b, hy_f_w1, hy_f_b1, hy_f_w2, hy_f_b2, hy_f_w3, hy_f_b3, hy_f_freq, hy_f_wout, hy_skip, lru_conv_w, lru_conv_b, lru_gate_w, lru_gate_b, lru_lambda, w_branch, w_out, norm2_g, ffn_w_gu, ffn_w_down, final_g):
    B = x.shape[0]
    meta = jnp.broadcast_to(meta_tokens[None].astype(x.dtype), (B, N_META, D_MODEL))
    h = jnp.concatenate([meta, x], axis=1)
    T = h.shape[1]
    for l in range(DEPTH):
        p = norm_proj(h.reshape(B * T, D_MODEL), norm1_g[l], w_in[l].astype(BF16)).reshape(B, T, D_IN)
        y_hy = hyena_mixer(p, hy_conv_w[l], hy_conv_b[l], hy_f_w1[l], hy_f_b1[l], hy_f_w2[l], hy_f_b2[l],
                           hy_f_w3[l], hy_f_b3[l], hy_f_freq[l], hy_f_wout[l], hy_skip[l])
        y_lru = rglru_mixer(p, lru_conv_w[l], lru_conv_b[l], lru_gate_w[l], lru_gate_b[l], lru_lambda[l])
        y_ret = retention_mixer(p)
        h = merge_out(h, y_hy, y_lru, y_ret, p, w_branch[l].astype(BF16), w_out[l].astype(BF16))
        h = ffn(h.reshape(B * T, D_MODEL), norm2_g[l], ffn_w_gu[l].astype(BF16), ffn_w_down[l].astype(BF16),
                final_g, final_norm=(l == DEPTH - 1)).reshape(B, T, D_MODEL)
    return h[:, N_META:]
```

```python
import functools
import math

import jax
import jax.numpy as jnp
import numpy as np
from jax import lax
from jax.experimental import pallas as pl
from jax.experimental.pallas import tpu as pltpu

F32 = jnp.float32
BF16 = jnp.bfloat16

D_MODEL = 1024
N_META = 16
EPS = 1e-6
DEPTH = 2

HY_WIDTH = D_MODEL
HY_ORDER = 2
HY_SHORT = 3
HY_BANDS = 16
HY_DECAY_PCT_SHORT = 0.3
HY_DECAY_PCT_LONG = 1.5
HY_DECAY_TARGET = 1e-2

LRU_WIDTH = D_MODEL
LRU_BLOCKS = 8
LRU_BLOCK = LRU_WIDTH // LRU_BLOCKS
LRU_CONV = 4
LRU_C = 8.0

RET_HEADS = 4
RET_DK = D_MODEL // RET_HEADS
RET_DV = D_MODEL // RET_HEADS
RET_WIDTH = RET_HEADS * RET_DV
RET_CHUNK = 128
ROPE_BASE = 10000.0

N_BRANCH = 3
D_FF = ((8 * D_MODEL + 3 * 256 - 1) // (3 * 256)) * 256

OFF_LRU_X = HY_ORDER * HY_WIDTH + HY_WIDTH
OFF_LRU_G = OFF_LRU_X + LRU_WIDTH
OFF_Q = OFF_LRU_G + LRU_WIDTH
OFF_K = OFF_Q + RET_HEADS * RET_DK
OFF_V = OFF_K + RET_HEADS * RET_DK
OFF_G = OFF_V + RET_WIDTH
OFF_MERGE = OFF_G + RET_WIDTH
D_IN = OFF_MERGE + N_BRANCH * D_MODEL

ROW_TILE = 912
MERGE_ROW_TILE = 432
VMEM_LIMIT = 56 * 1024 * 1024


def _rms(x, g):
    ms = jnp.mean(x * x, axis=-1, keepdims=True)
    return x * lax.rsqrt(ms + EPS) * g


def _sigmoid(x):
    return 0.5 * jnp.tanh(0.5 * x) + 0.5


def _norm_proj_kernel(h_ref, g_ref, w_ref, o_ref, u_scr):
    @pl.when(pl.program_id(1) == 0)
    def _():
        u_scr[...] = _rms(h_ref[...], g_ref[...]).astype(BF16)

    o_ref[...] = jnp.dot(u_scr[...], w_ref[...], preferred_element_type=F32).astype(o_ref.dtype)


def norm_proj(h2, g, w_bf16, layer, tn=2048):
    m, d = h2.shape
    n = w_bf16.shape[2]
    return pl.pallas_call(
        _norm_proj_kernel,
        grid=(m // ROW_TILE, n // tn),
        in_specs=[
            pl.BlockSpec((ROW_TILE, d), lambda i, j: (i, 0)),
            pl.BlockSpec((1, d), lambda i, j: (0, 0)),
            pl.BlockSpec((None, d, tn), lambda i, j: (layer, 0, j)),
        ],
        out_specs=pl.BlockSpec((ROW_TILE, tn), lambda i, j: (i, j)),
        out_shape=jax.ShapeDtypeStruct((m, n), BF16),
        scratch_shapes=[pltpu.VMEM((ROW_TILE, d), BF16)],
        compiler_params=pltpu.CompilerParams(
            dimension_semantics=("parallel", "arbitrary"), vmem_limit_bytes=VMEM_LIMIT),
        name="norm_proj",
    )(h2, g.reshape(1, d), w_bf16)


def _merge_kernel(h_ref, yh_ref, yl_ref, yr_ref, gate_ref, wb_ref, wo_ref, o_ref):
    merged = None
    for bi, y_ref in enumerate((yh_ref, yl_ref, yr_ref)):
        proj = jnp.dot(y_ref[...].astype(BF16), wb_ref[bi], preferred_element_type=F32)
        term = _sigmoid(gate_ref[:, bi * D_MODEL:(bi + 1) * D_MODEL].astype(F32)) * proj
        merged = term if merged is None else merged + term
    o_ref[...] = h_ref[...] + jnp.dot(merged.astype(BF16), wo_ref[...], preferred_element_type=F32)


def merge_out(h, y_hy, y_lru, y_ret, p, wb_bf16, wo_bf16, layer):
    B, T, d = h.shape
    tm = MERGE_ROW_TILE
    assert T % tm == 0
    row = lambda b, i: (b, i, 0)
    gate_blk = OFF_MERGE // (N_BRANCH * D_MODEL)
    blk = pl.BlockSpec((None, tm, d), row)
    return pl.pallas_call(
        _merge_kernel,
        grid=(B, T // tm),
        in_specs=[
            blk, blk, blk, blk,
            pl.BlockSpec((None, tm, N_BRANCH * D_MODEL), lambda b, i: (b, i, gate_blk)),
            pl.BlockSpec((None, N_BRANCH, d, d), lambda b, i: (layer, 0, 0, 0), pipeline_mode=pl.Buffered(1)),
            pl.BlockSpec((None, d, d), lambda b, i: (layer, 0, 0), pipeline_mode=pl.Buffered(1)),
        ],
        out_specs=blk,
        out_shape=jax.ShapeDtypeStruct((B, T, d), F32),
        compiler_params=pltpu.CompilerParams(
            dimension_semantics=("parallel", "parallel"), vmem_limit_bytes=VMEM_LIMIT),
        name="merge_out",
    )(h, y_hy, y_lru, y_ret, p, wb_bf16, wo_bf16)


FF_CHUNK = 256


def _ffn_kernel(h_ref, g_ref, wgu_ref, wd_ref, fg_ref, o_ref, v_scr, hid_scr, *, final_norm):
    v_scr[...] = _rms(h_ref[...], g_ref[...]).astype(BF16)
    for c in range(D_FF // FF_CHUNK):
        off = c * FF_CHUNK
        v = v_scr[...]
        gate = jnp.dot(v, wgu_ref[:, off:off + FF_CHUNK], preferred_element_type=F32)
        up = jnp.dot(v, wgu_ref[:, D_FF + off:D_FF + off + FF_CHUNK], preferred_element_type=F32)
        hid_scr[:, off:off + FF_CHUNK] = (gate * _sigmoid(gate) * up).astype(BF16)
    out = h_ref[...] + jnp.dot(hid_scr[...], wd_ref[...], preferred_element_type=F32)
    if final_norm:
        out = _rms(out, fg_ref[...])
    o_ref[...] = out


def ffn(h2, g, wgu_bf16, wd_bf16, final_g, layer, final_norm):
    m, d = h2.shape
    row = lambda i: (i, 0)
    return pl.pallas_call(
        functools.partial(_ffn_kernel, final_norm=final_norm),
        grid=(m // ROW_TILE,),
        in_specs=[
            pl.BlockSpec((ROW_TILE, d), row),
            pl.BlockSpec((1, d), lambda i: (0, 0)),
            pl.BlockSpec((None, d, 2 * D_FF), lambda i: (layer, 0, 0), pipeline_mode=pl.Buffered(1)),
            pl.BlockSpec((None, D_FF, d), lambda i: (layer, 0, 0), pipeline_mode=pl.Buffered(1)),
            pl.BlockSpec((1, d), lambda i: (0, 0)),
        ],
        out_specs=pl.BlockSpec((ROW_TILE, d), row),
        out_shape=jax.ShapeDtypeStruct((m, d), F32),
        scratch_shapes=[pltpu.VMEM((ROW_TILE, d), BF16), pltpu.VMEM((ROW_TILE, D_FF), BF16)],
        compiler_params=pltpu.CompilerParams(
            dimension_semantics=("parallel",), vmem_limit_bytes=VMEM_LIMIT),
        name="ffn",
    )(h2, g.reshape(1, d), wgu_bf16, wd_bf16, final_g.reshape(1, d))


LRU_ROWS = 216
COPY_ROWS = 432
SCAN_UNROLL = 6
HALO = 8


def _lru_kernel(x_ref, g_ref, cw_ref, cb_ref, gw_ref, gb_ref, lam_ref, o_ref,
                xs, gs, a0, b0, a1, b1, *, T):
    W = LRU_BLOCK
    m = T // 8
    zeros8 = jnp.zeros((HALO, W), F32)
    xs[pl.ds(0, HALO), :] = zeros8
    xs[pl.ds(HALO + T, HALO), :] = zeros8

    def copy_body(c, carry):
        r0 = pl.multiple_of(c * COPY_ROWS, 16)
        xs[pl.ds(HALO + r0, COPY_ROWS), :] = x_ref[pl.ds(r0, COPY_ROWS), :].astype(F32)
        gs[pl.ds(r0, COPY_ROWS), :] = g_ref[pl.ds(r0, COPY_ROWS), :].astype(F32)
        return carry

    lax.fori_loop(0, T // COPY_ROWS, copy_body, 0)

    cw = cw_ref[...]
    cb = cb_ref[...]
    gb = gb_ref[...]
    gw = gw_ref[...]
    nl = -lam_ref[...]
    sp = jnp.maximum(nl, 0.0) + jnp.log1p(jnp.exp(-jnp.abs(nl)))

    def gate_body(c, carry):
        r0 = pl.multiple_of(c * LRU_ROWS, 8)
        xc = cb
        for k in range(LRU_CONV):
            xc = xc + cw[k:k + 1] * xs[pl.ds(r0 + (HALO - LRU_CONV // 2 + k), LRU_ROWS), :]
        gates = _sigmoid(jnp.dot(xc.astype(BF16), gw, preferred_element_type=F32) + gb)
        for d, (a_s, b_s) in enumerate(((a0, b0), (a1, b1))):
            r = gates[:, (2 * d) * W:(2 * d + 1) * W]
            i = gates[:, (2 * d + 1) * W:(2 * d + 2) * W]
            log_a = (-LRU_C * sp[d:d + 1]) * r
            a = jnp.exp(log_a)
            a_s[pl.ds(r0, LRU_ROWS), :] = a
            b_s[pl.ds(r0, LRU_ROWS), :] = jnp.sqrt(1.0 - a * a) * i * xc
        return carry

    lax.fori_loop(0, T // LRU_ROWS, gate_body, 0, unroll=2)

    def scan_body(j, carry):
        hf, pf, hb, pb = carry
        rf = pl.ds(j, 8, stride=m)
        rb = pl.ds(m - 1 - j, 8, stride=m)
        af = a0[rf, :]
        hf = af * hf + b0[rf, :]
        pf = af * pf
        b0[rf, :] = hf
        a0[rf, :] = pf
        ab = a1[rb, :]
        hb = ab * hb + b1[rb, :]
        pb = ab * pb
        b1[rb, :] = hb
        a1[rb, :] = pb
        return hf, pf, hb, pb

    z = jnp.zeros((8, W), F32)
    o = jnp.ones((8, W), F32)
    hf, pf, hb, pb = lax.fori_loop(0, m, scan_body, (z, o, z, o), unroll=2)

    row = lax.broadcasted_iota(jnp.int32, (8, W), 0)
    cf = z
    cbk = z
    for _ in range(7):
        cf = jnp.where(row == 0, 0.0, pltpu.roll(hf + pf * cf, 1, axis=0))
        cbk = jnp.where(row == 7, 0.0, pltpu.roll(hb + pb * cbk, 7, axis=0))

    def out_body(j, carry):
        rows = pl.ds(j, 8, stride=m)
        h = b0[rows, :] + a0[rows, :] * cf + b1[rows, :] + a1[rows, :] * cbk
        o_ref[rows, :] = h * jax.nn.gelu(gs[rows, :])
        return carry

    lax.fori_loop(0, m, out_body, 0, unroll=SCAN_UNROLL)


def rglru_mixer(p, conv_w, conv_b, gate_w, gate_b, lam):
    B, T, _ = p.shape
    W = LRU_BLOCK
    assert T % LRU_ROWS == 0 and T % COPY_ROWS == 0
    gw = jnp.transpose(gate_w, (2, 3, 0, 1, 4)).reshape(LRU_BLOCKS, W, 4 * W).astype(BF16)
    gb = jnp.transpose(gate_b.reshape(2, 2, LRU_BLOCKS, W), (2, 0, 1, 3)).reshape(LRU_BLOCKS, 1, 4 * W)
    xoff = OFF_LRU_X // W
    goff = OFF_LRU_G // W
    seq = pltpu.VMEM((T, W), F32)
    return pl.pallas_call(
        functools.partial(_lru_kernel, T=T),
        grid=(B, LRU_BLOCKS),
        in_specs=[
            pl.BlockSpec((None, T, W), lambda b, n: (b, 0, xoff + n)),
            pl.BlockSpec((None, T, W), lambda b, n: (b, 0, goff + n)),
            pl.BlockSpec((LRU_CONV, W), lambda b, n: (0, n)),
            pl.BlockSpec((1, W), lambda b, n: (0, n)),
            pl.BlockSpec((None, W, 4 * W), lambda b, n: (n, 0, 0)),
            pl.BlockSpec((None, 1, 4 * W), lambda b, n: (n, 0, 0)),
            pl.BlockSpec((2, W), lambda b, n: (0, n)),
        ],
        out_specs=pl.BlockSpec((None, T, W), lambda b, n: (b, 0, n)),
        out_shape=jax.ShapeDtypeStruct((B, T, LRU_WIDTH), F32),
        scratch_shapes=[pltpu.VMEM((T + 2 * HALO, W), F32), seq, seq, seq, seq, seq],
        compiler_params=pltpu.CompilerParams(
            dimension_semantics=("parallel", "parallel"), vmem_limit_bytes=VMEM_LIMIT),
        name="rglru",
    )(p, p, conv_w, conv_b.reshape(1, -1), gw, gb, lam)


RET_ROWS = 432


def _ret_tables(T):
    C = RET_ROWS
    half = RET_DK // 2
    inv = ROPE_BASE ** (-np.linspace(0.0, 1.0, half, dtype=np.float32))
    ang = np.arange(T, dtype=np.float32)[:, None] * inv[None, :].astype(np.float32)
    cos = np.cos(ang.astype(np.float64)).astype(np.float32)
    sin = np.sin(ang.astype(np.float64)).astype(np.float32)
    log_gamma = np.log1p(-(2.0 ** (-5.0 - np.arange(RET_HEADS, dtype=np.float64))))
    idx = np.arange(C, dtype=np.float64)
    dmat = np.exp(np.abs(idx[:, None] - idx[None, :])[None] * log_gamma[:, None, None])
    sc = np.stack([np.exp((idx + 1.0)[:, None] * log_gamma[None]),
                   np.exp((C - 1.0 - idx)[:, None] * log_gamma[None]),
                   np.exp((C - idx)[:, None] * log_gamma[None]),
                   np.exp(idx[:, None] * log_gamma[None])])
    chunk_decay = np.exp(C * log_gamma)
    return (jnp.asarray(cos), jnp.asarray(sin), jnp.asarray(dmat, F32),
            jnp.asarray(sc, F32), [float(c) for c in chunk_decay])


def _rot(x, cos, sin):
    half = RET_DK // 2
    x1, x2 = x[:, :half], x[:, half:]
    return jnp.concatenate([x1 * cos - x2 * sin, x1 * sin + x2 * cos], axis=-1)


_TN = (((0,), (0,)), ((), ()))
_NT = (((1,), (1,)), ((), ()))


def _ret_fwd_kernel(q_ref, k_ref, v_ref, cos_ref, sin_ref, d_ref, sc_ref, o_ref, s_scr, *, decay):
    @pl.when(pl.program_id(1) == 0)
    def _():
        s_scr[...] = jnp.zeros_like(s_scr)

    cos = cos_ref[...]
    sin = sin_ref[...]
    for h in range(RET_HEADS):
        cols = slice(h * RET_DK, (h + 1) * RET_DK)
        q = _rot(q_ref[:, cols].astype(F32), cos, sin)
        k = _rot(k_ref[:, cols].astype(F32), cos, sin) * (RET_DK ** -0.5)
        v = v_ref[:, cols]
        scores = lax.dot_general(q.astype(BF16), k.astype(BF16), _NT, preferred_element_type=F32)
        scores = scores * d_ref[h]
        y = jnp.dot(scores.astype(BF16), v, preferred_element_type=F32)
        state = s_scr[h]
        qx = (q * sc_ref[0, :, h:h + 1]).astype(BF16)
        y = y + jnp.dot(qx, state.astype(BF16), preferred_element_type=F32)
        o_ref[:, cols] = y
        kz = (k * sc_ref[1, :, h:h + 1]).astype(BF16)
        s_scr[h] = decay[h] * state + lax.dot_general(kz, v, _TN, preferred_element_type=F32)


def _ret_bwd_kernel(q_ref, k_ref, v_ref, g_ref, y_ref, cos_ref, sin_ref, sc_ref, o_ref, s_scr, *, decay):
    @pl.when(pl.program_id(1) == 0)
    def _():
        s_scr[...] = jnp.zeros_like(s_scr)

    cos = cos_ref[...]
    sin = sin_ref[...]
    for h in range(RET_HEADS):
        cols = slice(h * RET_DK, (h + 1) * RET_DK)
        q = _rot(q_ref[:, cols].astype(F32), cos, sin)
        k = _rot(k_ref[:, cols].astype(F32), cos, sin) * (RET_DK ** -0.5)
        v = v_ref[:, cols]
        state = s_scr[h]
        qx = (q * sc_ref[2, :, h:h + 1]).astype(BF16)
        y = y_ref[:, cols] + jnp.dot(qx, state.astype(BF16), preferred_element_type=F32)
        kz = (k * sc_ref[3, :, h:h + 1]).astype(BF16)
        s_scr[h] = decay[h] * state + lax.dot_general(kz, v, _TN, preferred_element_type=F32)
        mu = jnp.mean(y, axis=-1, keepdims=True)
        yc = y - mu
        var = jnp.mean(yc * yc, axis=-1, keepdims=True)
        g = g_ref[:, cols].astype(F32)
        o_ref[:, cols] = (g * _sigmoid(g) * (yc * lax.rsqrt(var + EPS))).astype(o_ref.dtype)


def retention_mixer(p):
    B, T, _ = p.shape
    C = RET_ROWS
    assert T % C == 0
    n = T // C
    cos, sin, dmat, sc, decay = _ret_tables(T)
    half = RET_DK // 2
    qb, kb, vb, gb = (o // RET_WIDTH for o in (OFF_Q, OFF_K, OFF_V, OFF_G))
    state = pltpu.VMEM((RET_HEADS, RET_DK, RET_DV), F32)
    params = pltpu.CompilerParams(dimension_semantics=("parallel", "arbitrary"), vmem_limit_bytes=VMEM_LIMIT)

    def col(blk, rev):
        if rev:
            return pl.BlockSpec((None, C, RET_WIDTH), lambda b, i: (b, n - 1 - i, blk))
        return pl.BlockSpec((None, C, RET_WIDTH), lambda b, i: (b, i, blk))

    def tab(rev):
        if rev:
            return pl.BlockSpec((C, half), lambda b, i: (n - 1 - i, 0))
        return pl.BlockSpec((C, half), lambda b, i: (i, 0))

    const3 = lambda b, i: (0, 0, 0)
    y1 = pl.pallas_call(
        functools.partial(_ret_fwd_kernel, decay=decay),
        grid=(B, n),
        in_specs=[col(qb, False), col(kb, False), col(vb, False), tab(False), tab(False),
                  pl.BlockSpec((RET_HEADS, C, C), const3), pl.BlockSpec((4, C, RET_HEADS), const3)],
        out_specs=col(0, False),
        out_shape=jax.ShapeDtypeStruct((B, T, RET_WIDTH), F32),
        scratch_shapes=[state],
        compiler_params=params,
        name="ret_fwd",
    )(p, p, p, cos, sin, dmat, sc)
    return pl.pallas_call(
        functools.partial(_ret_bwd_kernel, decay=decay),
        grid=(B, n),
        in_specs=[col(qb, True), col(kb, True), col(vb, True), col(gb, True), col(0, True),
                  tab(True), tab(True), pl.BlockSpec((4, C, RET_HEADS), const3)],
        out_specs=col(0, True),
        out_shape=jax.ShapeDtypeStruct((B, T, RET_WIDTH), BF16),
        scratch_shapes=[state],
        compiler_params=params,
        name="ret_bwd",
    )(p, p, p, p, y1, cos, sin, sc)


FFT_N2 = 128


def _fft_dims(T):
    r = -(-T // FFT_N2)
    r = -(-r // 8) * 8
    return r, 2 * r


def _cblock(z):
    return np.block([[z.real, -z.imag], [z.imag, z.real]])


def _fft_tables(T):
    r, n1 = _fft_dims(T)
    n2 = FFT_N2
    n = n1 * n2
    a1 = np.arange(n1, dtype=np.float64)
    a2 = np.arange(n2, dtype=np.float64)
    f1 = np.exp(-2j * np.pi * np.outer(a1, a1) / n1)
    m1 = _cblock(f1[:, :r])
    m1f = np.concatenate([f1.real, f1.imag], axis=0)
    m1inv = _cblock(np.conj(f1.T)[:r, :] / n)
    f2 = np.exp(-2j * np.pi * np.outer(a2, a2) / n2)
    tw = np.exp(-2j * np.pi * np.outer(a1, a2) / n)
    g = f2[None, :, :] * tw[:, None, :]
    gi = np.conj(np.transpose(g, (0, 2, 1)))
    gblk = np.stack([_cblock(g[k]) for k in range(n1)])
    giblk = np.stack([_cblock(gi[k]) for k in range(n1)])
    cast = lambda a: jnp.asarray(a, F32).astype(BF16)
    return cast(m1), cast(m1f), cast(m1inv), cast(gblk), cast(giblk)


def _short_conv_kernel(x_ref, w_ref, b_ref, o_ref, xs, *, T, Tp):
    W = x_ref.shape[-1]
    zeros8 = jnp.zeros((HALO, W), F32)
    xs[pl.ds(0, HALO), :] = zeros8
    xs[pl.ds(HALO + T, HALO), :] = zeros8

    def copy_body(c, carry):
        r0 = pl.multiple_of(c * COPY_ROWS, 16)
        xs[pl.ds(HALO + r0, COPY_ROWS), :] = x_ref[pl.ds(r0, COPY_ROWS), :].astype(F32)
        return carry

    lax.fori_loop(0, T // COPY_ROWS, copy_body, 0)
    w = w_ref[...]
    b = b_ref[...]

    def conv_body(c, carry):
        r0 = pl.multiple_of(c * COPY_ROWS, 16)
        y = b
        for k in range(HY_SHORT):
            y = y + w[k:k + 1] * xs[pl.ds(r0 + (HALO - HY_SHORT // 2 + k), COPY_ROWS), :]
        o_ref[pl.ds(r0, COPY_ROWS), :] = y.astype(o_ref.dtype)
        return carry

    lax.fori_loop(0, T // COPY_ROWS, conv_body, 0)
    o_ref[pl.ds(T, Tp - T), :] = jnp.zeros((Tp - T, W), o_ref.dtype)


def hyena_short_conv(p, conv_w, conv_b, Tp):
    B, T, _ = p.shape
    W = 128
    ncol = (HY_ORDER + 1) * HY_WIDTH
    return pl.pallas_call(
        functools.partial(_short_conv_kernel, T=T, Tp=Tp),
        grid=(B, ncol // W),
        in_specs=[
            pl.BlockSpec((None, T, W), lambda b, n: (b, 0, n)),
            pl.BlockSpec((HY_SHORT, W), lambda b, n: (0, n)),
            pl.BlockSpec((1, W), lambda b, n: (0, n)),
        ],
        out_specs=pl.BlockSpec((None, Tp, W), lambda b, n: (b, 0, n)),
        out_shape=jax.ShapeDtypeStruct((B, Tp, ncol), BF16),
        scratch_shapes=[pltpu.VMEM((T + 2 * HALO, W), F32)],
        compiler_params=pltpu.CompilerParams(
            dimension_semantics=("parallel", "parallel"), vmem_limit_bytes=VMEM_LIMIT),
        name="hyena_short_conv",
    )(p, conv_w, conv_b.reshape(1, -1))


def _filter_tables(T):
    r, n1 = _fft_dims(T)
    n = n1 * FFT_N2
    idx = np.arange(n)
    fwd = idx < T
    bwd = idx > n - T
    lag = np.where(fwd, idx, np.where(bwd, n - idx, 0)).astype(np.float32)
    t_norm = lag / np.float32(max(T - 1, 1))
    bands = np.linspace(1e-4, HY_BANDS - 1, HY_BANDS, dtype=np.float32)
    ang = (np.float32(2.0 * math.pi / T) * lag[:, None]) * bands[None, :]
    feat = np.concatenate([t_norm[:, None], np.cos(ang), np.sin(ang)], axis=-1)
    valid = (fwd | bwd)[:, None]
    tab = np.zeros((FILT_COLS, n), np.float32)
    tab[:feat.shape[1]] = np.where(valid, feat, 0.0).T
    col = np.stack([fwd, bwd, t_norm], axis=1).astype(np.float32)
    max_decay = math.log(HY_DECAY_TARGET) / HY_DECAY_PCT_SHORT
    min_decay = math.log(HY_DECAY_TARGET) / HY_DECAY_PCT_LONG
    deltas = np.abs(np.linspace(min_decay, max_decay, HY_WIDTH, dtype=np.float32))
    return jnp.asarray(tab), jnp.asarray(col), jnp.asarray(deltas).reshape(1, -1)


FILT_COLS = 40
FILT_ROWS = 512
_HI = lax.Precision.HIGHEST


def _filter_kernel(tab_ref, col_ref, w1_ref, b1_ref, w2_ref, b2_ref, w3_ref, b3_ref, fr_ref, wo_ref, dl_ref,
                   o_ref, *, n_fwd, first_bwd):
    i = pl.program_id(0)

    def taps(direction):
        fr = fr_ref[...]
        hdn = tab_ref[...]
        for w_ref, b_ref in ((w1_ref, b1_ref), (w2_ref, b2_ref), (w3_ref, b3_ref)):
            hdn = jnp.sin(fr * (jnp.dot(w_ref[...], hdn, precision=_HI, preferred_element_type=F32) + b_ref[...]))
        hdn = hdn.astype(BF16)
        col = col_ref[...]
        window = jnp.exp(-col[:, 2:3] * dl_ref[...])
        for o in range(HY_ORDER):
            base = o * 2 * HY_WIDTH
            k = None
            for d in ((0, 1) if direction is None else (direction,)):
                kd = lax.dot_general(hdn, wo_ref[:, base + d * HY_WIDTH:base + (d + 1) * HY_WIDTH], _TN,
                                     preferred_element_type=F32)
                if direction is None:
                    kd = kd * col[:, d:d + 1]
                k = kd if k is None else k + kd
            o_ref[o] = (k * window).astype(o_ref.dtype)

    pl.when(i < n_fwd)(lambda: taps(0))
    pl.when(i >= first_bwd)(lambda: taps(1))
    pl.when((i >= n_fwd) & (i < first_bwd))(lambda: taps(None))


def hyena_filter(T, w1, b1, w2, b2, w3, b3, freq, wout):
    tab, col, deltas = _filter_tables(T)
    n = tab.shape[1]
    hid = w1.shape[1]
    w1t = jnp.zeros((hid, FILT_COLS), F32).at[:, :w1.shape[0]].set(w1.T)
    wout = wout.astype(BF16)
    full = lambda i: (0, 0)
    vec = lambda a: a.reshape(-1, 1)
    n_fwd = T // FILT_ROWS
    first_bwd = -(-(n - T + 1) // FILT_ROWS)
    return pl.pallas_call(
        functools.partial(_filter_kernel, n_fwd=n_fwd, first_bwd=first_bwd),
        grid=(n // FILT_ROWS,),
        in_specs=[
            pl.BlockSpec((FILT_COLS, FILT_ROWS), lambda i: (0, i)),
            pl.BlockSpec((FILT_ROWS, col.shape[1]), lambda i: (i, 0)),
            pl.BlockSpec(w1t.shape, full), pl.BlockSpec((hid, 1), full),
            pl.BlockSpec(w2.shape, full), pl.BlockSpec((hid, 1), full),
            pl.BlockSpec(w3.shape, full), pl.BlockSpec((hid, 1), full),
            pl.BlockSpec((hid, 1), full),
            pl.BlockSpec(wout.shape, full),
            pl.BlockSpec((1, HY_WIDTH), full),
        ],
        out_specs=pl.BlockSpec((HY_ORDER, FILT_ROWS, HY_WIDTH), lambda i: (0, i, 0)),
        out_shape=jax.ShapeDtypeStruct((HY_ORDER, n, HY_WIDTH), BF16),
        compiler_params=pltpu.CompilerParams(
            dimension_semantics=("parallel",), vmem_limit_bytes=VMEM_LIMIT),
        name="hyena_filter",
    )(tab, col, w1t, vec(b1), w2.T, vec(b2), w3.T, vec(b3), vec(freq), wout, deltas)


SUB = 8
COLS = 16
LANES = 128
WIDE = 256
SLABS = (WIDE // LANES) * (COLS // SUB)
K1_STEP = 4


def _columns_matmul(x_ref, m, xs, ys):
    rows_in = x_ref.shape[0] * x_ref.shape[1]
    rows_out = m.shape[0]
    lane_parts = []
    for lc in range(WIDE // LANES):
        x = x_ref[:, :, :, lc * LANES:(lc + 1) * LANES].astype(F32).reshape(rows_in, COLS, LANES)
        halves = []
        for half in range(COLS // SUB):
            s = lc * (COLS // SUB) + half
            xs[s] = x[:, half * SUB:(half + 1) * SUB, :].reshape(rows_in * SUB, LANES)
            for j in range(SUB):
                xj = xs[s, pl.ds(j, rows_in, stride=SUB), :].astype(BF16)
                ys[s, pl.ds(j, rows_out, stride=SUB), :] = jnp.dot(m, xj, preferred_element_type=F32)
            halves.append(ys[s].reshape(rows_out, SUB, LANES))
        lane_parts.append(jnp.concatenate(halves, axis=1))
    return jnp.concatenate(lane_parts, axis=2)


def _dft_rows_kernel(x_ref, m_ref, o_ref, xs, ys):
    y = _columns_matmul(x_ref, m_ref[...], xs, ys)
    o_ref[...] = y.reshape(o_ref.shape).astype(o_ref.dtype)


def dft_stage1(x, m, col_block, n1):
    G, nb, r, n2, _ = x.shape
    C = HY_WIDTH
    ct = C // WIDE
    return pl.pallas_call(
        _dft_rows_kernel,
        grid=(G, n2 // COLS, ct),
        in_specs=[
            pl.BlockSpec((None, nb, r, COLS, WIDE), lambda g, j, c: (g, 0, 0, j, col_block * ct + c)),
            pl.BlockSpec(m.shape, lambda g, j, c: (0, 0)),
        ],
        out_specs=pl.BlockSpec((None, 2, n1, COLS, WIDE), lambda g, j, c: (g, 0, 0, j, c)),
        out_shape=jax.ShapeDtypeStruct((G, 2, n1, n2, C), BF16),
        scratch_shapes=[pltpu.VMEM((SLABS, nb * r * SUB, LANES), F32),
                        pltpu.VMEM((SLABS, 2 * n1 * SUB, LANES), F32)],
        compiler_params=pltpu.CompilerParams(
            dimension_semantics=("parallel", "parallel", "parallel"), vmem_limit_bytes=VMEM_LIMIT),
        name="dft_stage1",
    )(x, m)


def _filter_spec_kernel(y_ref, g_ref, o_ref):
    for k in range(y_ref.shape[1]):
        y = y_ref[:, k].reshape(2 * FFT_N2, -1)
        z = jnp.dot(g_ref[k], y, preferred_element_type=F32)
        o_ref[:, k] = z.reshape(2, FFT_N2, -1).astype(o_ref.dtype)


def filter_spectrum(yf, gblk):
    G, _, n1, n2, C = yf.shape
    ks = K1_STEP if n1 % K1_STEP == 0 else 1
    blk = pl.BlockSpec((None, 2, ks, n2, C), lambda g, k: (g, 0, k, 0, 0))
    return pl.pallas_call(
        _filter_spec_kernel,
        grid=(G, n1 // ks),
        in_specs=[blk, pl.BlockSpec((ks, 2 * n2, 2 * n2), lambda g, k: (k, 0, 0))],
        out_specs=blk,
        out_shape=jax.ShapeDtypeStruct(yf.shape, BF16),
        compiler_params=pltpu.CompilerParams(
            dimension_semantics=("parallel", "parallel"), vmem_limit_bytes=VMEM_LIMIT),
        name="filter_spectrum",
    )(yf, gblk)


def _spectral_kernel(y_ref, kf_ref, g_ref, gi_ref, o_ref):
    for k in range(y_ref.shape[1]):
        y = y_ref[:, k].reshape(2 * FFT_N2, -1)
        z = jnp.dot(g_ref[k], y, preferred_element_type=F32)
        zr, zi = z[:FFT_N2], z[FFT_N2:]
        kr, ki = kf_ref[0, k].astype(F32), kf_ref[1, k].astype(F32)
        w = jnp.concatenate([zr * kr - zi * ki, zr * ki + zi * kr], axis=0).astype(BF16)
        v = jnp.dot(gi_ref[k], w, preferred_element_type=F32)
        o_ref[:, k] = v.reshape(2, FFT_N2, -1).astype(o_ref.dtype)


def spectral_stage(y, kf, order, gblk, giblk):
    _, n1, n2, C = y.shape
    ks = K1_STEP if n1 % K1_STEP == 0 else 1
    blk = pl.BlockSpec((2, ks, n2, C), lambda k: (0, k, 0, 0))
    mat = pl.BlockSpec((ks, 2 * n2, 2 * n2), lambda k: (k, 0, 0))
    return pl.pallas_call(
        _spectral_kernel,
        grid=(n1 // ks,),
        in_specs=[blk, pl.BlockSpec((None, 2, ks, n2, C), lambda k: (order, 0, k, 0, 0)), mat, mat],
        out_specs=blk,
        out_shape=jax.ShapeDtypeStruct(y.shape, BF16),
        compiler_params=pltpu.CompilerParams(
            dimension_semantics=("parallel",), vmem_limit_bytes=VMEM_LIMIT),
        name="hyena_spectral",
    )(y, kf, gblk, giblk)


def _idft_gate_kernel(v_ref, m_ref, gate_ref, z_ref, skip_ref, o_ref, xs, ys):
    conv = _columns_matmul(v_ref, m_ref[...], xs, ys).reshape(o_ref.shape)
    z = z_ref[...].astype(F32)
    o_ref[...] = (gate_ref[...].astype(F32) * (conv + z * skip_ref[...])).astype(o_ref.dtype)


def idft_gate(v, m1inv, uc5, gate_block, z5, z_block, skip):
    _, n1, n2, C = v.shape
    B, r = uc5.shape[0], uc5.shape[1]
    ct = C // WIDE
    rows = lambda blk: pl.BlockSpec((B, r, COLS, WIDE), lambda j, c: (0, 0, j, blk * ct + c))
    return pl.pallas_call(
        _idft_gate_kernel,
        grid=(n2 // COLS, ct),
        in_specs=[
            pl.BlockSpec((2, n1, COLS, WIDE), lambda j, c: (0, 0, j, c)),
            pl.BlockSpec(m1inv.shape, lambda j, c: (0, 0)),
            rows(gate_block), rows(z_block),
            pl.BlockSpec((1, WIDE), lambda j, c: (0, c)),
        ],
        out_specs=rows(0),
        out_shape=jax.ShapeDtypeStruct((B, r, n2, C), BF16),
        scratch_shapes=[pltpu.VMEM((SLABS, 2 * n1 * SUB, LANES), F32),
                        pltpu.VMEM((SLABS, B * r * SUB, LANES), F32)],
        compiler_params=pltpu.CompilerParams(
            dimension_semantics=("parallel", "parallel"), vmem_limit_bytes=VMEM_LIMIT),
        name="hyena_idft_gate",
    )(v, m1inv, uc5, z5, skip.reshape(1, C))


def hyena_mixer(p, conv_w, conv_b, fw1, fb1, fw2, fb2, fw3, fb3, freq, wout, skip):
    B, T, _ = p.shape
    assert B == 2, "the two batch rows are packed as one complex sequence"
    r, n1 = _fft_dims(T)
    n2 = FFT_N2
    Tp = r * n2
    m1, m1f, m1inv, gblk, giblk = _fft_tables(T)
    kc = hyena_filter(T, fw1, fb1, fw2, fb2, fw3, fb3, freq, wout)
    yf = dft_stage1(kc.reshape(HY_ORDER, 1, n1, n2, HY_WIDTH), m1f, 0, n1)
    kf = filter_spectrum(yf, gblk)
    uc = hyena_short_conv(p, conv_w, conv_b, Tp)
    uc4 = uc.reshape(B, r, n2, (HY_ORDER + 1) * HY_WIDTH)
    z4, zblk = uc4, HY_ORDER
    for o in range(HY_ORDER):
        y = dft_stage1(z4[None], m1, zblk, n1)[0]
        v = spectral_stage(y, kf, o, gblk, giblk)
        z4, zblk = idft_gate(v, m1inv, uc4, o, z4, zblk, skip[o]), 0
    return z4.reshape(B, Tp, HY_WIDTH)


def kernel(x, meta_tokens, norm1_g, w_in, hy_conv_w, hy_conv_b, hy_f_w1, hy_f_b1, hy_f_w2, hy_f_b2, hy_f_w3, hy_f_b3, hy_f_freq, hy_f_wout, hy_skip, lru_conv_w, lru_conv_b, lru_gate_w, lru_gate_b, lru_lambda, w_branch, w_out, norm2_g, ffn_w_gu, ffn_w_down, final_g):
    B = x.shape[0]
    meta = jnp.broadcast_to(meta_tokens[None].astype(x.dtype), (B, N_META, D_MODEL))
    h = jnp.concatenate([meta, x], axis=1)
    T = h.shape[1]
    w_in, w_branch, w_out, ffn_w_gu, ffn_w_down = (
        w.astype(BF16) for w in (w_in, w_branch, w_out, ffn_w_gu, ffn_w_down))
    for l in range(DEPTH):
        p = norm_proj(h.reshape(B * T, D_MODEL), norm1_g[l], w_in, l).reshape(B, T, D_IN)
        y_hy = hyena_mixer(p, hy_conv_w[l], hy_conv_b[l], hy_f_w1[l], hy_f_b1[l], hy_f_w2[l], hy_f_b2[l],
                           hy_f_w3[l], hy_f_b3[l], hy_f_freq[l], hy_f_wout[l], hy_skip[l])
        y_lru = rglru_mixer(p, lru_conv_w[l], lru_conv_b[l], lru_gate_w[l], lru_gate_b[l], lru_lambda[l])
        y_ret = retention_mixer(p)
        h = merge_out(h, y_hy, y_lru, y_ret, p, w_branch, w_out, l)
        h = ffn(h.reshape(B * T, D_MODEL), norm2_g[l], ffn_w_gu, ffn_w_down,
                final_g, l, final_norm=(l == DEPTH - 1)).reshape(B, T, D_MODEL)
    return h[:, N_META:]
```

```python
import functools
import math

import jax
import jax.numpy as jnp
import numpy as np
from jax import lax
from jax.experimental import pallas as pl
from jax.experimental.pallas import tpu as pltpu

F32 = jnp.float32
BF16 = jnp.bfloat16

D_MODEL = 1024
N_META = 16
EPS = 1e-6
DEPTH = 2

HY_WIDTH = D_MODEL
HY_ORDER = 2
HY_SHORT = 3
HY_BANDS = 16
HY_DECAY_PCT_SHORT = 0.3
HY_DECAY_PCT_LONG = 1.5
HY_DECAY_TARGET = 1e-2

LRU_WIDTH = D_MODEL
LRU_BLOCKS = 8
LRU_BLOCK = LRU_WIDTH // LRU_BLOCKS
LRU_CONV = 4
LRU_C = 8.0

RET_HEADS = 4
RET_DK = D_MODEL // RET_HEADS
RET_DV = D_MODEL // RET_HEADS
RET_WIDTH = RET_HEADS * RET_DV
RET_CHUNK = 128
ROPE_BASE = 10000.0

N_BRANCH = 3
D_FF = ((8 * D_MODEL + 3 * 256 - 1) // (3 * 256)) * 256

OFF_LRU_X = HY_ORDER * HY_WIDTH + HY_WIDTH
OFF_LRU_G = OFF_LRU_X + LRU_WIDTH
OFF_Q = OFF_LRU_G + LRU_WIDTH
OFF_K = OFF_Q + RET_HEADS * RET_DK
OFF_V = OFF_K + RET_HEADS * RET_DK
OFF_G = OFF_V + RET_WIDTH
OFF_MERGE = OFF_G + RET_WIDTH
D_IN = OFF_MERGE + N_BRANCH * D_MODEL

ROW_TILE = 912
MERGE_ROW_TILE = 432
VMEM_LIMIT = 56 * 1024 * 1024


def _rms(x, g):
    ms = jnp.mean(x * x, axis=-1, keepdims=True)
    return x * lax.rsqrt(ms + EPS) * g


def _sigmoid(x):
    return 0.5 * jnp.tanh(0.5 * x) + 0.5


def _norm_proj_kernel(h_ref, g_ref, w_ref, o_ref, u_scr):
    @pl.when(pl.program_id(1) == 0)
    def _():
        u_scr[...] = _rms(h_ref[...], g_ref[...]).astype(BF16)

    o_ref[...] = jnp.dot(u_scr[...], w_ref[...], preferred_element_type=F32).astype(o_ref.dtype)


def norm_proj(h2, g, w_bf16, layer, tn=2048):
    m, d = h2.shape
    n = w_bf16.shape[2]
    return pl.pallas_call(
        _norm_proj_kernel,
        grid=(m // ROW_TILE, n // tn),
        in_specs=[
            pl.BlockSpec((ROW_TILE, d), lambda i, j: (i, 0)),
            pl.BlockSpec((1, d), lambda i, j: (0, 0)),
            pl.BlockSpec((None, d, tn), lambda i, j: (layer, 0, j)),
        ],
        out_specs=pl.BlockSpec((ROW_TILE, tn), lambda i, j: (i, j)),
        out_shape=jax.ShapeDtypeStruct((m, n), BF16),
        scratch_shapes=[pltpu.VMEM((ROW_TILE, d), BF16)],
        compiler_params=pltpu.CompilerParams(
            dimension_semantics=("parallel", "arbitrary"), vmem_limit_bytes=VMEM_LIMIT),
        name="norm_proj",
    )(h2, g.reshape(1, d), w_bf16)


def _merge_kernel(h_ref, yh_ref, yl_ref, yr_ref, gate_ref, wb_ref, wo_ref, o_ref):
    merged = None
    for bi, y_ref in enumerate((yh_ref, yl_ref, yr_ref)):
        proj = jnp.dot(y_ref[...].astype(BF16), wb_ref[bi], preferred_element_type=F32)
        term = _sigmoid(gate_ref[:, bi * D_MODEL:(bi + 1) * D_MODEL].astype(F32)) * proj
        merged = term if merged is None else merged + term
    o_ref[...] = h_ref[...] + jnp.dot(merged.astype(BF16), wo_ref[...], preferred_element_type=F32)


def merge_out(h, y_hy, y_lru, y_ret, p, wb_bf16, wo_bf16, layer):
    B, T, d = h.shape
    tm = MERGE_ROW_TILE
    assert T % tm == 0
    row = lambda b, i: (b, i, 0)
    gate_blk = OFF_MERGE // (N_BRANCH * D_MODEL)
    blk = pl.BlockSpec((None, tm, d), row)
    return pl.pallas_call(
        _merge_kernel,
        grid=(B, T // tm),
        in_specs=[
            blk, blk, blk, blk,
            pl.BlockSpec((None, tm, N_BRANCH * D_MODEL), lambda b, i: (b, i, gate_blk)),
            pl.BlockSpec((None, N_BRANCH, d, d), lambda b, i: (layer, 0, 0, 0), pipeline_mode=pl.Buffered(1)),
            pl.BlockSpec((None, d, d), lambda b, i: (layer, 0, 0), pipeline_mode=pl.Buffered(1)),
        ],
        out_specs=blk,
        out_shape=jax.ShapeDtypeStruct((B, T, d), F32),
        compiler_params=pltpu.CompilerParams(
            dimension_semantics=("parallel", "parallel"), vmem_limit_bytes=VMEM_LIMIT),
        name="merge_out",
    )(h, y_hy, y_lru, y_ret, p, wb_bf16, wo_bf16)


FF_CHUNK = 256


def _ffn_kernel(h_ref, g_ref, wgu_ref, wd_ref, fg_ref, o_ref, v_scr, hid_scr, *, final_norm):
    v_scr[...] = _rms(h_ref[...], g_ref[...]).astype(BF16)
    for c in range(D_FF // FF_CHUNK):
        off = c * FF_CHUNK
        v = v_scr[...]
        gate = jnp.dot(v, wgu_ref[:, off:off + FF_CHUNK], preferred_element_type=F32)
        up = jnp.dot(v, wgu_ref[:, D_FF + off:D_FF + off + FF_CHUNK], preferred_element_type=F32)
        hid_scr[:, off:off + FF_CHUNK] = (gate * _sigmoid(gate) * up).astype(BF16)
    out = h_ref[...] + jnp.dot(hid_scr[...], wd_ref[...], preferred_element_type=F32)
    if final_norm:
        out = _rms(out, fg_ref[...])
    o_ref[...] = out


def ffn(h2, g, wgu_bf16, wd_bf16, final_g, layer, final_norm):
    m, d = h2.shape
    row = lambda i: (i, 0)
    return pl.pallas_call(
        functools.partial(_ffn_kernel, final_norm=final_norm),
        grid=(m // ROW_TILE,),
        in_specs=[
            pl.BlockSpec((ROW_TILE, d), row),
            pl.BlockSpec((1, d), lambda i: (0, 0)),
            pl.BlockSpec((None, d, 2 * D_FF), lambda i: (layer, 0, 0), pipeline_mode=pl.Buffered(1)),
            pl.BlockSpec((None, D_FF, d), lambda i: (layer, 0, 0), pipeline_mode=pl.Buffered(1)),
            pl.BlockSpec((1, d), lambda i: (0, 0)),
        ],
        out_specs=pl.BlockSpec((ROW_TILE, d), row),
        out_shape=jax.ShapeDtypeStruct((m, d), F32),
        scratch_shapes=[pltpu.VMEM((ROW_TILE, d), BF16), pltpu.VMEM((ROW_TILE, D_FF), BF16)],
        compiler_params=pltpu.CompilerParams(
            dimension_semantics=("parallel",), vmem_limit_bytes=VMEM_LIMIT),
        name="ffn",
    )(h2, g.reshape(1, d), wgu_bf16, wd_bf16, final_g.reshape(1, d))


LRU_ROWS = 216
COPY_ROWS = 432
SEG = 16
SCAN_UNROLL = 3
HALO = 8


def _lru_kernel(x_ref, g_ref, cw_ref, cb_ref, gw_ref, gb_ref, lam_ref, o_ref,
                xs, gs, a0, b0, a1, b1, *, T):
    W = LRU_BLOCK
    m = T // SEG
    zeros8 = jnp.zeros((HALO, W), F32)
    xs[pl.ds(0, HALO), :] = zeros8
    xs[pl.ds(HALO + T, HALO), :] = zeros8

    def copy_body(c, carry):
        r0 = pl.multiple_of(c * COPY_ROWS, 16)
        xs[pl.ds(HALO + r0, COPY_ROWS), :] = x_ref[pl.ds(r0, COPY_ROWS), :].astype(F32)
        gs[pl.ds(r0, COPY_ROWS), :] = g_ref[pl.ds(r0, COPY_ROWS), :].astype(F32)
        return carry

    lax.fori_loop(0, T // COPY_ROWS, copy_body, 0)

    cw = cw_ref[...]
    cb = cb_ref[...]
    gb = gb_ref[...]
    gw = gw_ref[...]
    nl = -lam_ref[...]
    sp = jnp.maximum(nl, 0.0) + jnp.log1p(jnp.exp(-jnp.abs(nl)))

    def gate_body(c, carry):
        r0 = pl.multiple_of(c * LRU_ROWS, 8)
        xc = cb
        for k in range(LRU_CONV):
            xc = xc + cw[k:k + 1] * xs[pl.ds(r0 + (HALO - LRU_CONV // 2 + k), LRU_ROWS), :]
        th = jnp.tanh(jnp.dot(xc.astype(BF16), gw, preferred_element_type=F32) + gb)
        half_xc = 0.5 * xc
        for d, (a_s, b_s) in enumerate(((a0, b0), (a1, b1))):
            tr = th[:, (2 * d) * W:(2 * d + 1) * W]
            ti = th[:, (2 * d + 1) * W:(2 * d + 2) * W]
            log_a = (-0.5 * LRU_C * sp[d:d + 1]) * (tr + 1.0)
            a = jnp.exp(log_a)
            a_s[pl.ds(r0, LRU_ROWS), :] = a
            b_s[pl.ds(r0, LRU_ROWS), :] = jnp.sqrt(1.0 - a * a) * ((ti + 1.0) * half_xc)
        return carry

    lax.fori_loop(0, T // LRU_ROWS, gate_body, 0, unroll=2)

    def scan_body(j, carry):
        hf, pf, hb, pb = carry
        rf = pl.ds(j, SEG, stride=m)
        rb = pl.ds(m - 1 - j, SEG, stride=m)
        af = a0[rf, :]
        hf = af * hf + b0[rf, :]
        pf = af * pf
        b0[rf, :] = hf
        a0[rf, :] = pf
        ab = a1[rb, :]
        hb = ab * hb + b1[rb, :]
        pb = ab * pb
        b1[rb, :] = hb
        a1[rb, :] = pb
        return hf, pf, hb, pb

    z = jnp.zeros((SEG, W), F32)
    o = jnp.ones((SEG, W), F32)
    hf, pf, hb, pb = lax.fori_loop(0, m, scan_body, (z, o, z, o))

    row = lax.broadcasted_iota(jnp.int32, (SEG, W), 0)
    cf = z
    cbk = z
    for _ in range(SEG - 1):
        cf = jnp.where(row == 0, 0.0, pltpu.roll(hf + pf * cf, 1, axis=0))
        cbk = jnp.where(row == SEG - 1, 0.0, pltpu.roll(hb + pb * cbk, SEG - 1, axis=0))

    def out_body(j, carry):
        rows = pl.ds(j, SEG, stride=m)
        h = b0[rows, :] + a0[rows, :] * cf + b1[rows, :] + a1[rows, :] * cbk
        o_ref[rows, :] = h * jax.nn.gelu(gs[rows, :])
        return carry

    lax.fori_loop(0, m, out_body, 0, unroll=SCAN_UNROLL)


def rglru_mixer(p, conv_w, conv_b, gate_w, gate_b, lam):
    B, T, _ = p.shape
    W = LRU_BLOCK
    assert T % LRU_ROWS == 0 and T % COPY_ROWS == 0 and T % SEG == 0
    gw = (0.5 * jnp.transpose(gate_w, (2, 3, 0, 1, 4))).reshape(LRU_BLOCKS, W, 4 * W).astype(BF16)
    gb = 0.5 * jnp.transpose(gate_b.reshape(2, 2, LRU_BLOCKS, W), (2, 0, 1, 3)).reshape(LRU_BLOCKS, 1, 4 * W)
    xoff = OFF_LRU_X // W
    goff = OFF_LRU_G // W
    seq = pltpu.VMEM((T, W), F32)
    return pl.pallas_call(
        functools.partial(_lru_kernel, T=T),
        grid=(B, LRU_BLOCKS),
        in_specs=[
            pl.BlockSpec((None, T, W), lambda b, n: (b, 0, xoff + n)),
            pl.BlockSpec((None, T, W), lambda b, n: (b, 0, goff + n)),
            pl.BlockSpec((LRU_CONV, W), lambda b, n: (0, n)),
            pl.BlockSpec((1, W), lambda b, n: (0, n)),
            pl.BlockSpec((None, W, 4 * W), lambda b, n: (n, 0, 0)),
            pl.BlockSpec((None, 1, 4 * W), lambda b, n: (n, 0, 0)),
            pl.BlockSpec((2, W), lambda b, n: (0, n)),
        ],
        out_specs=pl.BlockSpec((None, T, W), lambda b, n: (b, 0, n)),
        out_shape=jax.ShapeDtypeStruct((B, T, LRU_WIDTH), F32),
        scratch_shapes=[pltpu.VMEM((T + 2 * HALO, W), F32), seq, seq, seq, seq, seq],
        compiler_params=pltpu.CompilerParams(
            dimension_semantics=("parallel", "parallel"), vmem_limit_bytes=VMEM_LIMIT),
        name="rglru",
    )(p, p, conv_w, conv_b.reshape(1, -1), gw, gb, lam)


RET_ROWS = 432


def _ret_tables(T):
    C = RET_ROWS
    half = RET_DK // 2
    inv = ROPE_BASE ** (-np.linspace(0.0, 1.0, half, dtype=np.float32))
    ang = np.arange(T, dtype=np.float32)[:, None] * inv[None, :].astype(np.float32)
    cos = np.cos(ang.astype(np.float64)).astype(np.float32)
    sin = np.sin(ang.astype(np.float64)).astype(np.float32)
    log_gamma = np.log1p(-(2.0 ** (-5.0 - np.arange(RET_HEADS, dtype=np.float64))))
    idx = np.arange(C, dtype=np.float64)
    dmat = np.exp(np.abs(idx[:, None] - idx[None, :])[None] * log_gamma[:, None, None])
    sc = np.stack([np.exp((idx + 1.0)[:, None] * log_gamma[None]),
                   np.exp((C - 1.0 - idx)[:, None] * log_gamma[None]),
                   np.exp((C - idx)[:, None] * log_gamma[None]),
                   np.exp(idx[:, None] * log_gamma[None])])
    chunk_decay = np.exp(C * log_gamma)
    return (jnp.asarray(cos), jnp.asarray(sin), jnp.asarray(dmat, F32),
            jnp.asarray(sc, F32), [float(c) for c in chunk_decay])


def _rot(x, cos, sin):
    half = RET_DK // 2
    x1, x2 = x[:, :half], x[:, half:]
    return jnp.concatenate([x1 * cos - x2 * sin, x1 * sin + x2 * cos], axis=-1)


_TN = (((0,), (0,)), ((), ()))
_NT = (((1,), (1,)), ((), ()))


def _ret_fwd_kernel(q_ref, k_ref, v_ref, cos_ref, sin_ref, d_ref, sc_ref, o_ref, s_scr, *, decay):
    @pl.when(pl.program_id(1) == 0)
    def _():
        s_scr[...] = jnp.zeros_like(s_scr)

    cos = cos_ref[...]
    sin = sin_ref[...]
    for h in range(RET_HEADS):
        cols = slice(h * RET_DK, (h + 1) * RET_DK)
        q = _rot(q_ref[:, cols].astype(F32), cos, sin)
        k = _rot(k_ref[:, cols].astype(F32), cos, sin) * (RET_DK ** -0.5)
        v = v_ref[:, cols]
        scores = lax.dot_general(q.astype(BF16), k.astype(BF16), _NT, preferred_element_type=F32)
        scores = scores * d_ref[h]
        y = jnp.dot(scores.astype(BF16), v, preferred_element_type=F32)
        state = s_scr[h]
        qx = (q * sc_ref[0, :, h:h + 1]).astype(BF16)
        y = y + jnp.dot(qx, state.astype(BF16), preferred_element_type=F32)
        o_ref[:, cols] = y
        kz = (k * sc_ref[1, :, h:h + 1]).astype(BF16)
        s_scr[h] = decay[h] * state + lax.dot_general(kz, v, _TN, preferred_element_type=F32)


def _ret_bwd_kernel(q_ref, k_ref, v_ref, g_ref, y_ref, cos_ref, sin_ref, sc_ref, o_ref, s_scr, *, decay):
    @pl.when(pl.program_id(1) == 0)
    def _():
        s_scr[...] = jnp.zeros_like(s_scr)

    cos = cos_ref[...]
    sin = sin_ref[...]
    for h in range(RET_HEADS):
        cols = slice(h * RET_DK, (h + 1) * RET_DK)
        q = _rot(q_ref[:, cols].astype(F32), cos, sin)
        k = _rot(k_ref[:, cols].astype(F32), cos, sin) * (RET_DK ** -0.5)
        v = v_ref[:, cols]
        state = s_scr[h]
        qx = (q * sc_ref[2, :, h:h + 1]).astype(BF16)
        y = y_ref[:, cols] + jnp.dot(qx, state.astype(BF16), preferred_element_type=F32)
        kz = (k * sc_ref[3, :, h:h + 1]).astype(BF16)
        s_scr[h] = decay[h] * state + lax.dot_general(kz, v, _TN, preferred_element_type=F32)
        mu = jnp.mean(y, axis=-1, keepdims=True)
        yc = y - mu
        var = jnp.mean(yc * yc, axis=-1, keepdims=True)
        g = g_ref[:, cols].astype(F32)
        o_ref[:, cols] = (g * _sigmoid(g) * (yc * lax.rsqrt(var + EPS))).astype(o_ref.dtype)


def retention_mixer(p):
    B, T, _ = p.shape
    C = RET_ROWS
    assert T % C == 0
    n = T // C
    cos, sin, dmat, sc, decay = _ret_tables(T)
    half = RET_DK // 2
    qb, kb, vb, gb = (o // RET_WIDTH for o in (OFF_Q, OFF_K, OFF_V, OFF_G))
    state = pltpu.VMEM((RET_HEADS, RET_DK, RET_DV), F32)
    params = pltpu.CompilerParams(dimension_semantics=("parallel", "arbitrary"), vmem_limit_bytes=VMEM_LIMIT)

    def col(blk, rev):
        if rev:
            return pl.BlockSpec((None, C, RET_WIDTH), lambda b, i: (b, n - 1 - i, blk))
        return pl.BlockSpec((None, C, RET_WIDTH), lambda b, i: (b, i, blk))

    def tab(rev):
        if rev:
            return pl.BlockSpec((C, half), lambda b, i: (n - 1 - i, 0))
        return pl.BlockSpec((C, half), lambda b, i: (i, 0))

    const3 = lambda b, i: (0, 0, 0)
    y1 = pl.pallas_call(
        functools.partial(_ret_fwd_kernel, decay=decay),
        grid=(B, n),
        in_specs=[col(qb, False), col(kb, False), col(vb, False), tab(False), tab(False),
                  pl.BlockSpec((RET_HEADS, C, C), const3), pl.BlockSpec((4, C, RET_HEADS), const3)],
        out_specs=col(0, False),
        out_shape=jax.ShapeDtypeStruct((B, T, RET_WIDTH), F32),
        scratch_shapes=[state],
        compiler_params=params,
        name="ret_fwd",
    )(p, p, p, cos, sin, dmat, sc)
    return pl.pallas_call(
        functools.partial(_ret_bwd_kernel, decay=decay),
        grid=(B, n),
        in_specs=[col(qb, True), col(kb, True), col(vb, True), col(gb, True), col(0, True),
                  tab(True), tab(True), pl.BlockSpec((4, C, RET_HEADS), const3)],
        out_specs=col(0, True),
        out_shape=jax.ShapeDtypeStruct((B, T, RET_WIDTH), BF16),
        scratch_shapes=[state],
        compiler_params=params,
        name="ret_bwd",
    )(p, p, p, p, y1, cos, sin, sc)


FFT_N2 = 128


def _fft_dims(T):
    r = -(-T // FFT_N2)
    r = -(-r // 8) * 8
    return r, 2 * r


def _half_rows(n1):
    return -(-(n1 // 2 + 1) // 8) * 8


def _cblock(z):
    return np.block([[z.real, -z.imag], [z.imag, z.real]])


def _fft_tables(T):
    r, n1 = _fft_dims(T)
    n2 = FFT_N2
    n = n1 * n2
    a1 = np.arange(n1, dtype=np.float64)
    a2 = np.arange(n2, dtype=np.float64)
    f1 = np.exp(-2j * np.pi * np.outer(a1, a1) / n1)
    m1 = _cblock(f1[:, :r])
    kh = _half_rows(n1)
    m1f = np.concatenate([f1.real[:kh], f1.imag[:kh]], axis=0)
    m1inv = _cblock(np.conj(f1.T)[:r, :] / n)
    f2 = np.exp(-2j * np.pi * np.outer(a2, a2) / n2)
    tw = np.exp(-2j * np.pi * np.outer(a1, a2) / n)
    g = f2[None, :, :] * tw[:, None, :]
    gi = np.conj(np.transpose(g, (0, 2, 1)))
    gblk = np.stack([_cblock(g[k]) for k in range(n1)])
    giblk = np.stack([_cblock(gi[k]) for k in range(n1)])
    cast = lambda a: jnp.asarray(a, F32).astype(BF16)
    return cast(m1), cast(m1f), cast(m1inv), cast(gblk), cast(giblk)


def _short_conv_kernel(x_ref, w_ref, b_ref, o_ref, xs, *, T, Tp):
    W = x_ref.shape[-1]
    zeros8 = jnp.zeros((HALO, W), F32)
    xs[pl.ds(0, HALO), :] = zeros8
    xs[pl.ds(HALO + T, HALO), :] = zeros8

    def copy_body(c, carry):
        r0 = pl.multiple_of(c * COPY_ROWS, 16)
        xs[pl.ds(HALO + r0, COPY_ROWS), :] = x_ref[pl.ds(r0, COPY_ROWS), :].astype(F32)
        return carry

    lax.fori_loop(0, T // COPY_ROWS, copy_body, 0)
    w = w_ref[...]
    b = b_ref[...]

    def conv_body(c, carry):
        r0 = pl.multiple_of(c * COPY_ROWS, 16)
        y = b
        for k in range(HY_SHORT):
            y = y + w[k:k + 1] * xs[pl.ds(r0 + (HALO - HY_SHORT // 2 + k), COPY_ROWS), :]
        o_ref[pl.ds(r0, COPY_ROWS), :] = y.astype(o_ref.dtype)
        return carry

    lax.fori_loop(0, T // COPY_ROWS, conv_body, 0)
    o_ref[pl.ds(T, Tp - T), :] = jnp.zeros((Tp - T, W), o_ref.dtype)


def hyena_short_conv(p, conv_w, conv_b, Tp):
    B, T, _ = p.shape
    W = 128
    ncol = (HY_ORDER + 1) * HY_WIDTH
    return pl.pallas_call(
        functools.partial(_short_conv_kernel, T=T, Tp=Tp),
        grid=(B, ncol // W),
        in_specs=[
            pl.BlockSpec((None, T, W), lambda b, n: (b, 0, n)),
            pl.BlockSpec((HY_SHORT, W), lambda b, n: (0, n)),
            pl.BlockSpec((1, W), lambda b, n: (0, n)),
        ],
        out_specs=pl.BlockSpec((None, Tp, W), lambda b, n: (b, 0, n)),
        out_shape=jax.ShapeDtypeStruct((B, Tp, ncol), BF16),
        scratch_shapes=[pltpu.VMEM((T + 2 * HALO, W), F32)],
        compiler_params=pltpu.CompilerParams(
            dimension_semantics=("parallel", "parallel"), vmem_limit_bytes=VMEM_LIMIT),
        name="hyena_short_conv",
    )(p, conv_w, conv_b.reshape(1, -1))


def _filter_tables(T):
    r, n1 = _fft_dims(T)
    n = n1 * FFT_N2
    idx = np.arange(n)
    fwd = idx < T
    bwd = idx > n - T
    lag = np.where(fwd, idx, np.where(bwd, n - idx, 0)).astype(np.float32)
    t_norm = lag / np.float32(max(T - 1, 1))
    bands = np.linspace(1e-4, HY_BANDS - 1, HY_BANDS, dtype=np.float32)
    ang = (np.float32(2.0 * math.pi / T) * lag[:, None]) * bands[None, :]
    feat = np.concatenate([t_norm[:, None], np.cos(ang), np.sin(ang)], axis=-1)
    valid = (fwd | bwd)[:, None]
    tab = np.zeros((FILT_COLS, n), np.float32)
    tab[:feat.shape[1]] = np.where(valid, feat, 0.0).T
    col = np.stack([fwd, bwd, t_norm], axis=1).astype(np.float32)
    max_decay = math.log(HY_DECAY_TARGET) / HY_DECAY_PCT_SHORT
    min_decay = math.log(HY_DECAY_TARGET) / HY_DECAY_PCT_LONG
    deltas = np.abs(np.linspace(min_decay, max_decay, HY_WIDTH, dtype=np.float32))
    return jnp.asarray(tab), jnp.asarray(col), jnp.asarray(deltas).reshape(1, -1)


FILT_COLS = 40
FILT_ROWS = 512
_HI = lax.Precision.HIGHEST


def _filter_kernel(tab_ref, col_ref, w1_ref, b1_ref, w2_ref, b2_ref, w3_ref, b3_ref, fr_ref, wo_ref, dl_ref,
                   o_ref, *, n_fwd, first_bwd):
    i = pl.program_id(0)

    def taps(direction):
        fr = fr_ref[...]
        hdn = tab_ref[...]
        for w_ref, b_ref in ((w1_ref, b1_ref), (w2_ref, b2_ref), (w3_ref, b3_ref)):
            hdn = jnp.sin(fr * (jnp.dot(w_ref[...], hdn, precision=_HI, preferred_element_type=F32) + b_ref[...]))
        hdn = hdn.astype(BF16)
        col = col_ref[...]
        window = jnp.exp(-col[:, 2:3] * dl_ref[...])
        for o in range(HY_ORDER):
            base = o * 2 * HY_WIDTH
            k = None
            for d in ((0, 1) if direction is None else (direction,)):
                kd = lax.dot_general(hdn, wo_ref[:, base + d * HY_WIDTH:base + (d + 1) * HY_WIDTH], _TN,
                                     preferred_element_type=F32)
                if direction is None:
                    kd = kd * col[:, d:d + 1]
                k = kd if k is None else k + kd
            o_ref[o] = (k * window).astype(o_ref.dtype)

    pl.when(i < n_fwd)(lambda: taps(0))
    pl.when(i >= first_bwd)(lambda: taps(1))
    pl.when((i >= n_fwd) & (i < first_bwd))(lambda: taps(None))


def hyena_filter(T, w1, b1, w2, b2, w3, b3, freq, wout):
    tab, col, deltas = _filter_tables(T)
    n = tab.shape[1]
    hid = w1.shape[1]
    w1t = jnp.zeros((hid, FILT_COLS), F32).at[:, :w1.shape[0]].set(w1.T)
    wout = wout.astype(BF16)
    full = lambda i: (0, 0)
    vec = lambda a: a.reshape(-1, 1)
    n_fwd = T // FILT_ROWS
    first_bwd = -(-(n - T + 1) // FILT_ROWS)
    return pl.pallas_call(
        functools.partial(_filter_kernel, n_fwd=n_fwd, first_bwd=first_bwd),
        grid=(n // FILT_ROWS,),
        in_specs=[
            pl.BlockSpec((FILT_COLS, FILT_ROWS), lambda i: (0, i)),
            pl.BlockSpec((FILT_ROWS, col.shape[1]), lambda i: (i, 0)),
            pl.BlockSpec(w1t.shape, full), pl.BlockSpec((hid, 1), full),
            pl.BlockSpec(w2.shape, full), pl.BlockSpec((hid, 1), full),
            pl.BlockSpec(w3.shape, full), pl.BlockSpec((hid, 1), full),
            pl.BlockSpec((hid, 1), full),
            pl.BlockSpec(wout.shape, full),
            pl.BlockSpec((1, HY_WIDTH), full),
        ],
        out_specs=pl.BlockSpec((HY_ORDER, FILT_ROWS, HY_WIDTH), lambda i: (0, i, 0)),
        out_shape=jax.ShapeDtypeStruct((HY_ORDER, n, HY_WIDTH), BF16),
        compiler_params=pltpu.CompilerParams(
            dimension_semantics=("parallel",), vmem_limit_bytes=VMEM_LIMIT),
        name="hyena_filter",
    )(tab, col, w1t, vec(b1), w2.T, vec(b2), w3.T, vec(b3), vec(freq), wout, deltas)


SUB = 8
COLS = 16
LANES = 128
WIDE = 256
SLABS = (WIDE // LANES) * (COLS // SUB)
K1_STEP = 4


def _columns_matmul(x_ref, m, xs, ys):
    rows_in = x_ref.shape[0] * x_ref.shape[1]
    rows_out = m.shape[0]
    lane_parts = []
    for lc in range(WIDE // LANES):
        x = x_ref[:, :, :, lc * LANES:(lc + 1) * LANES].astype(F32).reshape(rows_in, COLS, LANES)
        halves = []
        for half in range(COLS // SUB):
            s = lc * (COLS // SUB) + half
            xs[s] = x[:, half * SUB:(half + 1) * SUB, :].reshape(rows_in * SUB, LANES)
            for j in range(SUB):
                xj = xs[s, pl.ds(j, rows_in, stride=SUB), :].astype(BF16)
                ys[s, pl.ds(j, rows_out, stride=SUB), :] = jnp.dot(m, xj, preferred_element_type=F32)
            halves.append(ys[s].reshape(rows_out, SUB, LANES))
        lane_parts.append(jnp.concatenate(halves, axis=1))
    return jnp.concatenate(lane_parts, axis=2)


def _dft_rows_kernel(x_ref, m_ref, o_ref, xs, ys):
    y = _columns_matmul(x_ref, m_ref[...], xs, ys)
    o_ref[...] = y.reshape(o_ref.shape).astype(o_ref.dtype)


def dft_stage1(x, m, col_block):
    G, nb, r, n2, _ = x.shape
    C = HY_WIDTH
    ct = C // WIDE
    n1 = m.shape[0] // 2
    return pl.pallas_call(
        _dft_rows_kernel,
        grid=(G, n2 // COLS, ct),
        in_specs=[
            pl.BlockSpec((None, nb, r, COLS, WIDE), lambda g, j, c: (g, 0, 0, j, col_block * ct + c)),
            pl.BlockSpec(m.shape, lambda g, j, c: (0, 0)),
        ],
        out_specs=pl.BlockSpec((None, 2, n1, COLS, WIDE), lambda g, j, c: (g, 0, 0, j, c)),
        out_shape=jax.ShapeDtypeStruct((G, 2, n1, n2, C), BF16),
        scratch_shapes=[pltpu.VMEM((SLABS, nb * r * SUB, LANES), F32),
                        pltpu.VMEM((SLABS, 2 * n1 * SUB, LANES), F32)],
        compiler_params=pltpu.CompilerParams(
            dimension_semantics=("parallel", "parallel", "parallel"), vmem_limit_bytes=VMEM_LIMIT),
        name="dft_stage1",
    )(x, m)


def _spectral_kernel(y_ref, *refs, n1):
    ks = y_ref.shape[1]
    yf_refs, (g_ref, gi_ref, o_ref) = refs[:ks], refs[ks:]
    for k in range(ks):
        g = g_ref[k]
        z = jnp.dot(g, y_ref[:, k].reshape(2 * FFT_N2, -1), preferred_element_type=F32)
        k1 = pl.program_id(0) * ks + k
        sign = jnp.where(2 * k1 > n1, -1.0, 1.0).astype(BF16)
        yf = jnp.concatenate([yf_refs[k][0], yf_refs[k][1] * sign], axis=0)
        kf = jnp.dot(g, yf, preferred_element_type=F32)
        zr, zi = z[:FFT_N2], z[FFT_N2:]
        kr, ki = kf[:FFT_N2], kf[FFT_N2:]
        w = jnp.concatenate([zr * kr - zi * ki, zr * ki + zi * kr], axis=0).astype(BF16)
        v = jnp.dot(gi_ref[k], w, preferred_element_type=F32)
        o_ref[:, k] = v.reshape(2, FFT_N2, -1).astype(o_ref.dtype)


def spectral_stage(y, yf, order, gblk, giblk):
    _, n1, n2, C = y.shape
    ks = K1_STEP if n1 % K1_STEP == 0 else 1
    blk = pl.BlockSpec((2, ks, n2, C), lambda k: (0, k, 0, 0))
    mat = pl.BlockSpec((ks, 2 * n2, 2 * n2), lambda k: (k, 0, 0))

    def filter_row(i):
        def index(k):
            k1 = k * ks + i
            return (order, 0, jnp.where(2 * k1 > n1, n1 - k1, k1), 0, 0)
        return pl.BlockSpec((None, 2, None, n2, C), index)

    return pl.pallas_call(
        functools.partial(_spectral_kernel, n1=n1),
        grid=(n1 // ks,),
        in_specs=[blk] + [filter_row(i) for i in range(ks)] + [mat, mat],
        out_specs=blk,
        out_shape=jax.ShapeDtypeStruct(y.shape, BF16),
        compiler_params=pltpu.CompilerParams(
            dimension_semantics=("parallel",), vmem_limit_bytes=VMEM_LIMIT),
        name="hyena_spectral",
    )(y, *([yf] * ks), gblk, giblk)


def _idft_gate_kernel(v_ref, m_ref, gate_ref, z_ref, skip_ref, o_ref, xs, ys):
    conv = _columns_matmul(v_ref, m_ref[...], xs, ys).reshape(o_ref.shape)
    z = z_ref[...].astype(F32)
    o_ref[...] = (gate_ref[...].astype(F32) * (conv + z * skip_ref[...])).astype(o_ref.dtype)


def idft_gate(v, m1inv, uc5, gate_block, z5, z_block, skip):
    _, n1, n2, C = v.shape
    B, r = uc5.shape[0], uc5.shape[1]
    ct = C // WIDE
    rows = lambda blk: pl.BlockSpec((B, r, COLS, WIDE), lambda j, c: (0, 0, j, blk * ct + c))
    return pl.pallas_call(
        _idft_gate_kernel,
        grid=(n2 // COLS, ct),
        in_specs=[
            pl.BlockSpec((2, n1, COLS, WIDE), lambda j, c: (0, 0, j, c)),
            pl.BlockSpec(m1inv.shape, lambda j, c: (0, 0)),
            rows(gate_block), rows(z_block),
            pl.BlockSpec((1, WIDE), lambda j, c: (0, c)),
        ],
        out_specs=rows(0),
        out_shape=jax.ShapeDtypeStruct((B, r, n2, C), BF16),
        scratch_shapes=[pltpu.VMEM((SLABS, 2 * n1 * SUB, LANES), F32),
                        pltpu.VMEM((SLABS, B * r * SUB, LANES), F32)],
        compiler_params=pltpu.CompilerParams(
            dimension_semantics=("parallel", "parallel"), vmem_limit_bytes=VMEM_LIMIT),
        name="hyena_idft_gate",
    )(v, m1inv, uc5, z5, skip.reshape(1, C))


def hyena_mixer(p, conv_w, conv_b, fw1, fb1, fw2, fb2, fw3, fb3, freq, wout, skip):
    B, T, _ = p.shape
    assert B == 2, "the two batch rows are packed as one complex sequence"
    r, n1 = _fft_dims(T)
    n2 = FFT_N2
    Tp = r * n2
    m1, m1f, m1inv, gblk, giblk = _fft_tables(T)
    kc = hyena_filter(T, fw1, fb1, fw2, fb2, fw3, fb3, freq, wout)
    yf = dft_stage1(kc.reshape(HY_ORDER, 1, n1, n2, HY_WIDTH), m1f, 0)
    uc = hyena_short_conv(p, conv_w, conv_b, Tp)
    uc4 = uc.reshape(B, r, n2, (HY_ORDER + 1) * HY_WIDTH)
    z4, zblk = uc4, HY_ORDER
    for o in range(HY_ORDER):
        y = dft_stage1(z4[None], m1, zblk)[0]
        v = spectral_stage(y, yf, o, gblk, giblk)
        z4, zblk = idft_gate(v, m1inv, uc4, o, z4, zblk, skip[o]), 0
    return z4.reshape(B, Tp, HY_WIDTH)


def kernel(x, meta_tokens, norm1_g, w_in, hy_conv_w, hy_conv_b, hy_f_w1, hy_f_b1, hy_f_w2, hy_f_b2, hy_f_w3, hy_f_b3, hy_f_freq, hy_f_wout, hy_skip, lru_conv_w, lru_conv_b, lru_gate_w, lru_gate_b, lru_lambda, w_branch, w_out, norm2_g, ffn_w_gu, ffn_w_down, final_g):
    B = x.shape[0]
    meta = jnp.broadcast_to(meta_tokens[None].astype(x.dtype), (B, N_META, D_MODEL))
    h = jnp.concatenate([meta, x], axis=1)
    T = h.shape[1]
    w_in, w_branch, w_out, ffn_w_gu, ffn_w_down = (
        w.astype(BF16) for w in (w_in, w_branch, w_out, ffn_w_gu, ffn_w_down))
    for l in range(DEPTH):
        p = norm_proj(h.reshape(B * T, D_MODEL), norm1_g[l], w_in, l).reshape(B, T, D_IN)
        y_hy = hyena_mixer(p, hy_conv_w[l], hy_conv_b[l], hy_f_w1[l], hy_f_b1[l], hy_f_w2[l], hy_f_b2[l],
                           hy_f_w3[l], hy_f_b3[l], hy_f_freq[l], hy_f_wout[l], hy_skip[l])
        y_lru = rglru_mixer(p, lru_conv_w[l], lru_conv_b[l], lru_gate_w[l], lru_gate_b[l], lru_lambda[l])
        y_ret = retention_mixer(p)
        h = merge_out(h, y_hy, y_lru, y_ret, p, w_branch, w_out, l)
        h = ffn(h.reshape(B * T, D_MODEL), norm2_g[l], ffn_w_gu, ffn_w_down,
                final_g, l, final_norm=(l == DEPTH - 1)).reshape(B, T, D_MODEL)
    return h[:, N_META:]
```

```python
import functools
import math

import jax
import jax.numpy as jnp
import numpy as np
from jax import lax
from jax.experimental import pallas as pl
from jax.experimental.pallas import tpu as pltpu

F32 = jnp.float32
BF16 = jnp.bfloat16

D_MODEL = 1024
N_META = 16
EPS = 1e-6
DEPTH = 2

HY_WIDTH = D_MODEL
HY_ORDER = 2
HY_SHORT = 3
HY_BANDS = 16
HY_DECAY_PCT_SHORT = 0.3
HY_DECAY_PCT_LONG = 1.5
HY_DECAY_TARGET = 1e-2

LRU_WIDTH = D_MODEL
LRU_BLOCKS = 8
LRU_BLOCK = LRU_WIDTH // LRU_BLOCKS
LRU_CONV = 4
LRU_C = 8.0

RET_HEADS = 4
RET_DK = D_MODEL // RET_HEADS
RET_DV = D_MODEL // RET_HEADS
RET_WIDTH = RET_HEADS * RET_DV
RET_CHUNK = 128
ROPE_BASE = 10000.0

N_BRANCH = 3
D_FF = ((8 * D_MODEL + 3 * 256 - 1) // (3 * 256)) * 256

OFF_LRU_X = HY_ORDER * HY_WIDTH + HY_WIDTH
OFF_LRU_G = OFF_LRU_X + LRU_WIDTH
OFF_Q = OFF_LRU_G + LRU_WIDTH
OFF_K = OFF_Q + RET_HEADS * RET_DK
OFF_V = OFF_K + RET_HEADS * RET_DK
OFF_G = OFF_V + RET_WIDTH
OFF_MERGE = OFF_G + RET_WIDTH
D_IN = OFF_MERGE + N_BRANCH * D_MODEL

ROW_TILE = 912
MERGE_ROW_TILE = 432
VMEM_LIMIT = 56 * 1024 * 1024


def _rms(x, g):
    ms = jnp.mean(x * x, axis=-1, keepdims=True)
    return x * lax.rsqrt(ms + EPS) * g


def _sigmoid(x):
    return 0.5 * jnp.tanh(0.5 * x) + 0.5


def _norm_proj_kernel(h_ref, g_ref, w_ref, o_ref, u_scr):
    @pl.when(pl.program_id(1) == 0)
    def _():
        u_scr[...] = _rms(h_ref[...], g_ref[...]).astype(BF16)

    o_ref[...] = jnp.dot(u_scr[...], w_ref[...], preferred_element_type=F32).astype(o_ref.dtype)


def norm_proj(h2, g, w_bf16, layer, tn=2048):
    m, d = h2.shape
    n = w_bf16.shape[2]
    return pl.pallas_call(
        _norm_proj_kernel,
        grid=(m // ROW_TILE, n // tn),
        in_specs=[
            pl.BlockSpec((ROW_TILE, d), lambda i, j: (i, 0)),
            pl.BlockSpec((1, d), lambda i, j: (0, 0)),
            pl.BlockSpec((None, d, tn), lambda i, j: (layer, 0, j)),
        ],
        out_specs=pl.BlockSpec((ROW_TILE, tn), lambda i, j: (i, j)),
        out_shape=jax.ShapeDtypeStruct((m, n), BF16),
        scratch_shapes=[pltpu.VMEM((ROW_TILE, d), BF16)],
        compiler_params=pltpu.CompilerParams(
            dimension_semantics=("parallel", "arbitrary"), vmem_limit_bytes=VMEM_LIMIT),
        name="norm_proj",
    )(h2, g.reshape(1, d), w_bf16)


def _merge_kernel(h_ref, yh_ref, yl_ref, yr_ref, gate_ref, wb_ref, wo_ref, o_ref):
    merged = None
    for bi, y_ref in enumerate((yh_ref, yl_ref, yr_ref)):
        proj = jnp.dot(y_ref[...].astype(BF16), wb_ref[bi], preferred_element_type=F32)
        term = _sigmoid(gate_ref[:, bi * D_MODEL:(bi + 1) * D_MODEL].astype(F32)) * proj
        merged = term if merged is None else merged + term
    o_ref[...] = h_ref[...] + jnp.dot(merged.astype(BF16), wo_ref[...], preferred_element_type=F32)


def merge_out(h, y_hy, y_lru, y_ret, p, wb_bf16, wo_bf16, layer):
    B, T, d = h.shape
    tm = MERGE_ROW_TILE
    assert T % tm == 0
    row = lambda b, i: (b, i, 0)
    gate_blk = OFF_MERGE // (N_BRANCH * D_MODEL)
    blk = pl.BlockSpec((None, tm, d), row)
    return pl.pallas_call(
        _merge_kernel,
        grid=(B, T // tm),
        in_specs=[
            blk, blk, blk, blk,
            pl.BlockSpec((None, tm, N_BRANCH * D_MODEL), lambda b, i: (b, i, gate_blk)),
            pl.BlockSpec((None, N_BRANCH, d, d), lambda b, i: (layer, 0, 0, 0), pipeline_mode=pl.Buffered(1)),
            pl.BlockSpec((None, d, d), lambda b, i: (layer, 0, 0), pipeline_mode=pl.Buffered(1)),
        ],
        out_specs=blk,
        out_shape=jax.ShapeDtypeStruct((B, T, d), F32),
        compiler_params=pltpu.CompilerParams(
            dimension_semantics=("parallel", "parallel"), vmem_limit_bytes=VMEM_LIMIT),
        name="merge_out",
    )(h, y_hy, y_lru, y_ret, p, wb_bf16, wo_bf16)


FF_CHUNK = 256


def _ffn_kernel(h_ref, g_ref, wgu_ref, wd_ref, fg_ref, o_ref, v_scr, hid_scr, *, final_norm):
    v_scr[...] = _rms(h_ref[...], g_ref[...]).astype(BF16)
    for c in range(D_FF // FF_CHUNK):
        off = c * FF_CHUNK
        v = v_scr[...]
        gate = jnp.dot(v, wgu_ref[:, off:off + FF_CHUNK], preferred_element_type=F32)
        up = jnp.dot(v, wgu_ref[:, D_FF + off:D_FF + off + FF_CHUNK], preferred_element_type=F32)
        hid_scr[:, off:off + FF_CHUNK] = (gate * _sigmoid(gate) * up).astype(BF16)
    out = h_ref[...] + jnp.dot(hid_scr[...], wd_ref[...], preferred_element_type=F32)
    if final_norm:
        out = _rms(out, fg_ref[...])
    o_ref[...] = out


def ffn(h2, g, wgu_bf16, wd_bf16, final_g, layer, final_norm):
    m, d = h2.shape
    row = lambda i: (i, 0)
    return pl.pallas_call(
        functools.partial(_ffn_kernel, final_norm=final_norm),
        grid=(m // ROW_TILE,),
        in_specs=[
            pl.BlockSpec((ROW_TILE, d), row),
            pl.BlockSpec((1, d), lambda i: (0, 0)),
            pl.BlockSpec((None, d, 2 * D_FF), lambda i: (layer, 0, 0), pipeline_mode=pl.Buffered(1)),
            pl.BlockSpec((None, D_FF, d), lambda i: (layer, 0, 0), pipeline_mode=pl.Buffered(1)),
            pl.BlockSpec((1, d), lambda i: (0, 0)),
        ],
        out_specs=pl.BlockSpec((ROW_TILE, d), row),
        out_shape=jax.ShapeDtypeStruct((m, d), F32),
        scratch_shapes=[pltpu.VMEM((ROW_TILE, d), BF16), pltpu.VMEM((ROW_TILE, D_FF), BF16)],
        compiler_params=pltpu.CompilerParams(
            dimension_semantics=("parallel",), vmem_limit_bytes=VMEM_LIMIT),
        name="ffn",
    )(h2, g.reshape(1, d), wgu_bf16, wd_bf16, final_g.reshape(1, d))


FINAL_ROW_TILE = 1024


def ffn_final(h, g, wgu_bf16, wd_bf16, final_g, layer):
    B, T, d = h.shape
    seq = T - N_META
    tm = FINAL_ROW_TILE
    assert seq % tm == 0
    def body(h_ref, *rest):
        _ffn_kernel(h_ref.at[0], *rest, final_norm=True)

    return pl.pallas_call(
        body,
        grid=(B, seq // tm),
        in_specs=[
            pl.BlockSpec((pl.Element(1), pl.Element(tm), pl.Element(d)), lambda b, i: (b, pl.multiple_of(N_META + i * tm, 8), 0)),
            pl.BlockSpec((1, d), lambda b, i: (0, 0)),
            pl.BlockSpec((None, d, 2 * D_FF), lambda b, i: (layer, 0, 0), pipeline_mode=pl.Buffered(1)),
            pl.BlockSpec((None, D_FF, d), lambda b, i: (layer, 0, 0), pipeline_mode=pl.Buffered(1)),
            pl.BlockSpec((1, d), lambda b, i: (0, 0)),
        ],
        out_specs=pl.BlockSpec((None, tm, d), lambda b, i: (b, i, 0)),
        out_shape=jax.ShapeDtypeStruct((B, seq, d), F32),
        scratch_shapes=[pltpu.VMEM((tm, d), BF16), pltpu.VMEM((tm, D_FF), BF16)],
        compiler_params=pltpu.CompilerParams(
            dimension_semantics=("parallel", "parallel"), vmem_limit_bytes=VMEM_LIMIT),
        name="ffn_final",
    )(h, g.reshape(1, d), wgu_bf16, wd_bf16, final_g.reshape(1, d))


LRU_ROWS = 216
COPY_ROWS = 432
SEG = 16
SCAN_UNROLL = 3
HALO = 8


def _lru_kernel(x_ref, g_ref, cw_ref, cb_ref, gw_ref, gb_ref, lam_ref, o_ref,
                xs, gs, a0, b0, a1, b1, *, T):
    W = LRU_BLOCK
    m = T // SEG
    zeros8 = jnp.zeros((HALO, W), F32)
    xs[pl.ds(0, HALO), :] = zeros8
    xs[pl.ds(HALO + T, HALO), :] = zeros8

    def copy_body(c, carry):
        r0 = pl.multiple_of(c * COPY_ROWS, 16)
        xs[pl.ds(HALO + r0, COPY_ROWS), :] = x_ref[pl.ds(r0, COPY_ROWS), :].astype(F32)
        gs[pl.ds(r0, COPY_ROWS), :] = g_ref[pl.ds(r0, COPY_ROWS), :].astype(F32)
        return carry

    lax.fori_loop(0, T // COPY_ROWS, copy_body, 0)

    cw = cw_ref[...]
    cb = cb_ref[...]
    gb = gb_ref[...]
    gw = gw_ref[...]
    nl = -lam_ref[...]
    sp = jnp.maximum(nl, 0.0) + jnp.log1p(jnp.exp(-jnp.abs(nl)))

    def gate_body(c, carry):
        r0 = pl.multiple_of(c * LRU_ROWS, 8)
        xc = cb
        for k in range(LRU_CONV):
            xc = xc + cw[k:k + 1] * xs[pl.ds(r0 + (HALO - LRU_CONV // 2 + k), LRU_ROWS), :]
        th = jnp.tanh(jnp.dot(xc.astype(BF16), gw, preferred_element_type=F32) + gb)
        half_xc = 0.5 * xc
        for d, (a_s, b_s) in enumerate(((a0, b0), (a1, b1))):
            tr = th[:, (2 * d) * W:(2 * d + 1) * W]
            ti = th[:, (2 * d + 1) * W:(2 * d + 2) * W]
            log_a = (-0.5 * LRU_C * sp[d:d + 1]) * (tr + 1.0)
            a = jnp.exp(log_a)
            a_s[pl.ds(r0, LRU_ROWS), :] = a
            b_s[pl.ds(r0, LRU_ROWS), :] = jnp.sqrt(1.0 - a * a) * ((ti + 1.0) * half_xc)
        return carry

    lax.fori_loop(0, T // LRU_ROWS, gate_body, 0, unroll=2)

    def scan_body(j, carry):
        hf, pf, hb, pb = carry
        rf = pl.ds(j, SEG, stride=m)
        rb = pl.ds(m - 1 - j, SEG, stride=m)
        af = a0[rf, :]
        hf = af * hf + b0[rf, :]
        pf = af * pf
        b0[rf, :] = hf
        a0[rf, :] = pf
        ab = a1[rb, :]
        hb = ab * hb + b1[rb, :]
        pb = ab * pb
        b1[rb, :] = hb
        a1[rb, :] = pb
        return hf, pf, hb, pb

    z = jnp.zeros((SEG, W), F32)
    o = jnp.ones((SEG, W), F32)
    hf, pf, hb, pb = lax.fori_loop(0, m, scan_body, (z, o, z, o))

    row = lax.broadcasted_iota(jnp.int32, (SEG, W), 0)
    cf = z
    cbk = z
    for _ in range(SEG - 1):
        cf = jnp.where(row == 0, 0.0, pltpu.roll(hf + pf * cf, 1, axis=0))
        cbk = jnp.where(row == SEG - 1, 0.0, pltpu.roll(hb + pb * cbk, SEG - 1, axis=0))

    def out_body(j, carry):
        rows = pl.ds(j, SEG, stride=m)
        h = b0[rows, :] + a0[rows, :] * cf + b1[rows, :] + a1[rows, :] * cbk
        o_ref[rows, :] = h * jax.nn.gelu(gs[rows, :])
        return carry

    lax.fori_loop(0, m, out_body, 0, unroll=SCAN_UNROLL)


def rglru_mixer(p, conv_w, conv_b, gate_w, gate_b, lam):
    B, T, _ = p.shape
    W = LRU_BLOCK
    assert T % LRU_ROWS == 0 and T % COPY_ROWS == 0 and T % SEG == 0
    gw = (0.5 * jnp.transpose(gate_w, (2, 3, 0, 1, 4))).reshape(LRU_BLOCKS, W, 4 * W).astype(BF16)
    gb = 0.5 * jnp.transpose(gate_b.reshape(2, 2, LRU_BLOCKS, W), (2, 0, 1, 3)).reshape(LRU_BLOCKS, 1, 4 * W)
    xoff = OFF_LRU_X // W
    goff = OFF_LRU_G // W
    seq = pltpu.VMEM((T, W), F32)
    return pl.pallas_call(
        functools.partial(_lru_kernel, T=T),
        grid=(B, LRU_BLOCKS),
        in_specs=[
            pl.BlockSpec((None, T, W), lambda b, n: (b, 0, xoff + n)),
            pl.BlockSpec((None, T, W), lambda b, n: (b, 0, goff + n)),
            pl.BlockSpec((LRU_CONV, W), lambda b, n: (0, n)),
            pl.BlockSpec((1, W), lambda b, n: (0, n)),
            pl.BlockSpec((None, W, 4 * W), lambda b, n: (n, 0, 0)),
            pl.BlockSpec((None, 1, 4 * W), lambda b, n: (n, 0, 0)),
            pl.BlockSpec((2, W), lambda b, n: (0, n)),
        ],
        out_specs=pl.BlockSpec((None, T, W), lambda b, n: (b, 0, n)),
        out_shape=jax.ShapeDtypeStruct((B, T, LRU_WIDTH), F32),
        scratch_shapes=[pltpu.VMEM((T + 2 * HALO, W), F32), seq, seq, seq, seq, seq],
        compiler_params=pltpu.CompilerParams(
            dimension_semantics=("parallel", "parallel"), vmem_limit_bytes=VMEM_LIMIT),
        name="rglru",
    )(p, p, conv_w, conv_b.reshape(1, -1), gw, gb, lam)


RET_ROWS = 432


def _ret_tables(T):
    C = RET_ROWS
    half = RET_DK // 2
    inv = ROPE_BASE ** (-np.linspace(0.0, 1.0, half, dtype=np.float32))
    ang = np.arange(T, dtype=np.float32)[:, None] * inv[None, :].astype(np.float32)
    cos = np.cos(ang.astype(np.float64)).astype(np.float32)
    sin = np.sin(ang.astype(np.float64)).astype(np.float32)
    log_gamma = np.log1p(-(2.0 ** (-5.0 - np.arange(RET_HEADS, dtype=np.float64))))
    idx = np.arange(C, dtype=np.float64)
    dmat = np.exp(np.abs(idx[:, None] - idx[None, :])[None] * log_gamma[:, None, None])
    sc = np.stack([np.exp((idx + 1.0)[:, None] * log_gamma[None]),
                   np.exp((C - 1.0 - idx)[:, None] * log_gamma[None]),
                   np.exp((C - idx)[:, None] * log_gamma[None]),
                   np.exp(idx[:, None] * log_gamma[None])])
    chunk_decay = np.exp(C * log_gamma)
    return (jnp.asarray(cos), jnp.asarray(sin), jnp.asarray(dmat, F32),
            jnp.asarray(sc, F32), [float(c) for c in chunk_decay])


def _rot(x, cos, sin):
    half = RET_DK // 2
    x1, x2 = x[:, :half], x[:, half:]
    return jnp.concatenate([x1 * cos - x2 * sin, x1 * sin + x2 * cos], axis=-1)


_TN = (((0,), (0,)), ((), ()))
_NT = (((1,), (1,)), ((), ()))


def _ret_fwd_kernel(q_ref, k_ref, v_ref, cos_ref, sin_ref, d_ref, sc_ref, o_ref, s_scr, *, decay):
    @pl.when(pl.program_id(1) == 0)
    def _():
        s_scr[...] = jnp.zeros_like(s_scr)

    cos = cos_ref[...]
    sin = sin_ref[...]
    for h in range(RET_HEADS):
        cols = slice(h * RET_DK, (h + 1) * RET_DK)
        q = _rot(q_ref[:, cols].astype(F32), cos, sin)
        k = _rot(k_ref[:, cols].astype(F32), cos, sin) * (RET_DK ** -0.5)
        v = v_ref[:, cols]
        scores = lax.dot_general(q.astype(BF16), k.astype(BF16), _NT, preferred_element_type=F32)
        scores = scores * d_ref[h]
        y = jnp.dot(scores.astype(BF16), v, preferred_element_type=F32)
        state = s_scr[h]
        qx = (q * sc_ref[0, :, h:h + 1]).astype(BF16)
        y = y + jnp.dot(qx, state.astype(BF16), preferred_element_type=F32)
        o_ref[:, cols] = y
        kz = (k * sc_ref[1, :, h:h + 1]).astype(BF16)
        s_scr[h] = decay[h] * state + lax.dot_general(kz, v, _TN, preferred_element_type=F32)


def _ret_bwd_kernel(q_ref, k_ref, v_ref, g_ref, y_ref, cos_ref, sin_ref, sc_ref, o_ref, s_scr, *, decay):
    @pl.when(pl.program_id(1) == 0)
    def _():
        s_scr[...] = jnp.zeros_like(s_scr)

    cos = cos_ref[...]
    sin = sin_ref[...]
    for h in range(RET_HEADS):
        cols = slice(h * RET_DK, (h + 1) * RET_DK)
        q = _rot(q_ref[:, cols].astype(F32), cos, sin)
        k = _rot(k_ref[:, cols].astype(F32), cos, sin) * (RET_DK ** -0.5)
        v = v_ref[:, cols]
        state = s_scr[h]
        qx = (q * sc_ref[2, :, h:h + 1]).astype(BF16)
        y = y_ref[:, cols] + jnp.dot(qx, state.astype(BF16), preferred_element_type=F32)
        kz = (k * sc_ref[3, :, h:h + 1]).astype(BF16)
        s_scr[h] = decay[h] * state + lax.dot_general(kz, v, _TN, preferred_element_type=F32)
        mu = jnp.mean(y, axis=-1, keepdims=True)
        yc = y - mu
        var = jnp.mean(yc * yc, axis=-1, keepdims=True)
        g = g_ref[:, cols].astype(F32)
        o_ref[:, cols] = (g * _sigmoid(g) * (yc * lax.rsqrt(var + EPS))).astype(o_ref.dtype)


def retention_mixer(p):
    B, T, _ = p.shape
    C = RET_ROWS
    assert T % C == 0
    n = T // C
    cos, sin, dmat, sc, decay = _ret_tables(T)
    half = RET_DK // 2
    qb, kb, vb, gb = (o // RET_WIDTH for o in (OFF_Q, OFF_K, OFF_V, OFF_G))
    state = pltpu.VMEM((RET_HEADS, RET_DK, RET_DV), F32)
    params = pltpu.CompilerParams(dimension_semantics=("parallel", "arbitrary"), vmem_limit_bytes=VMEM_LIMIT)

    def col(blk, rev):
        if rev:
            return pl.BlockSpec((None, C, RET_WIDTH), lambda b, i: (b, n - 1 - i, blk))
        return pl.BlockSpec((None, C, RET_WIDTH), lambda b, i: (b, i, blk))

    def tab(rev):
        if rev:
            return pl.BlockSpec((C, half), lambda b, i: (n - 1 - i, 0))
        return pl.BlockSpec((C, half), lambda b, i: (i, 0))

    const3 = lambda b, i: (0, 0, 0)
    y1 = pl.pallas_call(
        functools.partial(_ret_fwd_kernel, decay=decay),
        grid=(B, n),
        in_specs=[col(qb, False), col(kb, False), col(vb, False), tab(False), tab(False),
                  pl.BlockSpec((RET_HEADS, C, C), const3), pl.BlockSpec((4, C, RET_HEADS), const3)],
        out_specs=col(0, False),
        out_shape=jax.ShapeDtypeStruct((B, T, RET_WIDTH), F32),
        scratch_shapes=[state],
        compiler_params=params,
        name="ret_fwd",
    )(p, p, p, cos, sin, dmat, sc)
    return pl.pallas_call(
        functools.partial(_ret_bwd_kernel, decay=decay),
        grid=(B, n),
        in_specs=[col(qb, True), col(kb, True), col(vb, True), col(gb, True), col(0, True),
                  tab(True), tab(True), pl.BlockSpec((4, C, RET_HEADS), const3)],
        out_specs=col(0, True),
        out_shape=jax.ShapeDtypeStruct((B, T, RET_WIDTH), BF16),
        scratch_shapes=[state],
        compiler_params=params,
        name="ret_bwd",
    )(p, p, p, p, y1, cos, sin, sc)


FFT_N2 = 128


def _fft_dims(T):
    r = -(-T // FFT_N2)
    r = -(-r // 8) * 8
    return r, 2 * r


def _half_rows(n1):
    return -(-(n1 // 2 + 1) // 8) * 8


def _cblock(z):
    return np.block([[z.real, -z.imag], [z.imag, z.real]])


def _fft_tables(T):
    r, n1 = _fft_dims(T)
    n2 = FFT_N2
    n = n1 * n2
    a1 = np.arange(n1, dtype=np.float64)
    a2 = np.arange(n2, dtype=np.float64)
    f1 = np.exp(-2j * np.pi * np.outer(a1, a1) / n1)
    m1 = _cblock(f1[:, :r])
    kh = _half_rows(n1)
    m1f = np.concatenate([f1.real[:kh], f1.imag[:kh]], axis=0)
    m1inv = _cblock(np.conj(f1.T)[:r, :] / n)
    f2 = np.exp(-2j * np.pi * np.outer(a2, a2) / n2)
    tw = np.exp(-2j * np.pi * np.outer(a1, a2) / n)
    g = f2[None, :, :] * tw[:, None, :]
    gi = np.conj(np.transpose(g, (0, 2, 1)))
    gblk = np.stack([_cblock(g[k]) for k in range(n1)])
    giblk = np.stack([_cblock(gi[k]) for k in range(n1)])
    cast = lambda a: jnp.asarray(a, F32).astype(BF16)
    return cast(m1), cast(m1f), cast(m1inv), cast(gblk), cast(giblk)


def _short_conv_kernel(x_ref, w_ref, b_ref, o_ref, xs, *, T, Tp):
    W = x_ref.shape[-1]
    slabs = W // LANES
    zeros8 = jnp.zeros((HALO, LANES), F32)
    for s in range(slabs):
        xs[s, pl.ds(0, HALO), :] = zeros8
        xs[s, pl.ds(HALO + T, HALO), :] = zeros8

    def copy_body(c, carry):
        r0 = pl.multiple_of(c * COPY_ROWS, 16)
        x = x_ref[pl.ds(r0, COPY_ROWS), :].astype(F32)
        for s in range(slabs):
            xs[s, pl.ds(HALO + r0, COPY_ROWS), :] = x[:, s * LANES:(s + 1) * LANES]
        return carry

    lax.fori_loop(0, T // COPY_ROWS, copy_body, 0)
    w = w_ref[...]
    b = b_ref[...]

    def conv_body(c, carry):
        r0 = pl.multiple_of(c * COPY_ROWS, 16)
        for s in range(slabs):
            lanes = slice(s * LANES, (s + 1) * LANES)
            y = b[:, lanes]
            for k in range(HY_SHORT):
                y = y + w[k:k + 1, lanes] * xs[s, pl.ds(r0 + (HALO - HY_SHORT // 2 + k), COPY_ROWS), :]
            o_ref[pl.ds(r0, COPY_ROWS), lanes] = y.astype(o_ref.dtype)
        return carry

    lax.fori_loop(0, T // COPY_ROWS, conv_body, 0)
    o_ref[pl.ds(T, Tp - T), :] = jnp.zeros((Tp - T, W), o_ref.dtype)


def hyena_short_conv(p, conv_w, conv_b, Tp):
    B, T, _ = p.shape
    W = WIDE
    ncol = (HY_ORDER + 1) * HY_WIDTH
    return pl.pallas_call(
        functools.partial(_short_conv_kernel, T=T, Tp=Tp),
        grid=(B, ncol // W),
        in_specs=[
            pl.BlockSpec((None, T, W), lambda b, n: (b, 0, n)),
            pl.BlockSpec((HY_SHORT, W), lambda b, n: (0, n)),
            pl.BlockSpec((1, W), lambda b, n: (0, n)),
        ],
        out_specs=pl.BlockSpec((None, Tp, W), lambda b, n: (b, 0, n)),
        out_shape=jax.ShapeDtypeStruct((B, Tp, ncol), BF16),
        scratch_shapes=[pltpu.VMEM((W // LANES, T + 2 * HALO, LANES), F32)],
        compiler_params=pltpu.CompilerParams(
            dimension_semantics=("parallel", "parallel"), vmem_limit_bytes=VMEM_LIMIT),
        name="hyena_short_conv",
    )(p, conv_w, conv_b.reshape(1, -1))


def _filter_tables(T):
    r, n1 = _fft_dims(T)
    n = n1 * FFT_N2
    idx = np.arange(n)
    fwd = idx < T
    bwd = idx > n - T
    lag = np.where(fwd, idx, np.where(bwd, n - idx, 0)).astype(np.float32)
    t_norm = lag / np.float32(max(T - 1, 1))
    bands = np.linspace(1e-4, HY_BANDS - 1, HY_BANDS, dtype=np.float32)
    ang = (np.float32(2.0 * math.pi / T) * lag[:, None]) * bands[None, :]
    feat = np.concatenate([t_norm[:, None], np.cos(ang), np.sin(ang)], axis=-1)
    valid = (fwd | bwd)[:, None]
    tab = np.zeros((FILT_COLS, n), np.float32)
    tab[:feat.shape[1]] = np.where(valid, feat, 0.0).T
    col = np.stack([fwd, bwd, t_norm], axis=1).astype(np.float32)
    max_decay = math.log(HY_DECAY_TARGET) / HY_DECAY_PCT_SHORT
    min_decay = math.log(HY_DECAY_TARGET) / HY_DECAY_PCT_LONG
    deltas = np.abs(np.linspace(min_decay, max_decay, HY_WIDTH, dtype=np.float32))
    return jnp.asarray(tab), jnp.asarray(col), jnp.asarray(deltas).reshape(1, -1)


FILT_COLS = 40
FILT_ROWS = 512
_HI = lax.Precision.HIGHEST


def _filter_kernel(tab_ref, col_ref, w1_ref, b1_ref, w2_ref, b2_ref, w3_ref, b3_ref, fr_ref, wo_ref, dl_ref,
                   o_ref, *, n_fwd, first_bwd, zero_lo, zero_hi):
    i = pl.program_id(0)

    def taps(direction):
        fr = fr_ref[...]
        hdn = tab_ref[...]
        for w_ref, b_ref in ((w1_ref, b1_ref), (w2_ref, b2_ref), (w3_ref, b3_ref)):
            hdn = jnp.sin(fr * (jnp.dot(w_ref[...], hdn, precision=_HI, preferred_element_type=F32) + b_ref[...]))
        hdn = hdn.astype(BF16)
        col = col_ref[...]
        window = jnp.exp(-col[:, 2:3] * dl_ref[...])
        for o in range(HY_ORDER):
            base = o * 2 * HY_WIDTH
            k = None
            for d in ((0, 1) if direction is None else (direction,)):
                kd = lax.dot_general(hdn, wo_ref[:, base + d * HY_WIDTH:base + (d + 1) * HY_WIDTH], _TN,
                                     preferred_element_type=F32)
                if direction is None:
                    kd = kd * col[:, d:d + 1]
                k = kd if k is None else k + kd
            o_ref[o] = (k * window).astype(o_ref.dtype)

    def zeros():
        o_ref[...] = jnp.zeros(o_ref.shape, o_ref.dtype)

    is_zero = (i >= zero_lo) & (i < zero_hi)
    pl.when(i < n_fwd)(lambda: taps(0))
    pl.when(i >= first_bwd)(lambda: taps(1))
    pl.when(is_zero)(zeros)
    pl.when((i >= n_fwd) & (i < first_bwd) & jnp.logical_not(is_zero))(lambda: taps(None))


def hyena_filter(T, w1, b1, w2, b2, w3, b3, freq, wout):
    tab, col, deltas = _filter_tables(T)
    n = tab.shape[1]
    hid = w1.shape[1]
    w1t = jnp.zeros((hid, FILT_COLS), F32).at[:, :w1.shape[0]].set(w1.T)
    wout = wout.astype(BF16)
    full = lambda i: (0, 0)
    vec = lambda a: a.reshape(-1, 1)
    n_fwd = T // FILT_ROWS
    first_bwd = -(-(n - T + 1) // FILT_ROWS)
    return pl.pallas_call(
        functools.partial(_filter_kernel, n_fwd=n_fwd, first_bwd=first_bwd,
                          zero_lo=-(-T // FILT_ROWS), zero_hi=(n - T + 1) // FILT_ROWS),
        grid=(n // FILT_ROWS,),
        in_specs=[
            pl.BlockSpec((FILT_COLS, FILT_ROWS), lambda i: (0, i)),
            pl.BlockSpec((FILT_ROWS, col.shape[1]), lambda i: (i, 0)),
            pl.BlockSpec(w1t.shape, full), pl.BlockSpec((hid, 1), full),
            pl.BlockSpec(w2.shape, full), pl.BlockSpec((hid, 1), full),
            pl.BlockSpec(w3.shape, full), pl.BlockSpec((hid, 1), full),
            pl.BlockSpec((hid, 1), full),
            pl.BlockSpec(wout.shape, full),
            pl.BlockSpec((1, HY_WIDTH), full),
        ],
        out_specs=pl.BlockSpec((HY_ORDER, FILT_ROWS, HY_WIDTH), lambda i: (0, i, 0)),
        out_shape=jax.ShapeDtypeStruct((HY_ORDER, n, HY_WIDTH), BF16),
        compiler_params=pltpu.CompilerParams(
            dimension_semantics=("parallel",), vmem_limit_bytes=VMEM_LIMIT),
        name="hyena_filter",
    )(tab, col, w1t, vec(b1), w2.T, vec(b2), w3.T, vec(b3), vec(freq), wout, deltas)


SUB = 8
COLS = 16
LANES = 128
WIDE = 256
SLABS = (WIDE // LANES) * (COLS // SUB)
K1_STEP = 4


def _columns_matmul(x_ref, m, xs, ys):
    rows_in = x_ref.shape[0] * x_ref.shape[1]
    rows_out = m.shape[0]
    lane_parts = []
    for lc in range(WIDE // LANES):
        x = x_ref[:, :, :, lc * LANES:(lc + 1) * LANES].astype(F32).reshape(rows_in, COLS, LANES)
        halves = []
        for half in range(COLS // SUB):
            s = lc * (COLS // SUB) + half
            xs[s] = x[:, half * SUB:(half + 1) * SUB, :].reshape(rows_in * SUB, LANES)
            for j in range(SUB):
                xj = xs[s, pl.ds(j, rows_in, stride=SUB), :].astype(BF16)
                ys[s, pl.ds(j, rows_out, stride=SUB), :] = jnp.dot(m, xj, preferred_element_type=F32)
            halves.append(ys[s].reshape(rows_out, SUB, LANES))
        lane_parts.append(jnp.concatenate(halves, axis=1))
    return jnp.concatenate(lane_parts, axis=2)


def _dft_rows_kernel(x_ref, m_ref, o_ref, xs, ys):
    y = _columns_matmul(x_ref, m_ref[...], xs, ys)
    o_ref[...] = y.reshape(o_ref.shape).astype(o_ref.dtype)


def dft_stage1(x, m, col_block):
    G, nb, r, n2, _ = x.shape
    C = HY_WIDTH
    ct = C // WIDE
    n1 = m.shape[0] // 2
    return pl.pallas_call(
        _dft_rows_kernel,
        grid=(G, n2 // COLS, ct),
        in_specs=[
            pl.BlockSpec((None, nb, r, COLS, WIDE), lambda g, j, c: (g, 0, 0, j, col_block * ct + c)),
            pl.BlockSpec(m.shape, lambda g, j, c: (0, 0)),
        ],
        out_specs=pl.BlockSpec((None, 2, n1, COLS, WIDE), lambda g, j, c: (g, 0, 0, j, c)),
        out_shape=jax.ShapeDtypeStruct((G, 2, n1, n2, C), BF16),
        scratch_shapes=[pltpu.VMEM((SLABS, nb * r * SUB, LANES), F32),
                        pltpu.VMEM((SLABS, 2 * n1 * SUB, LANES), F32)],
        compiler_params=pltpu.CompilerParams(
            dimension_semantics=("parallel", "parallel", "parallel"), vmem_limit_bytes=VMEM_LIMIT),
        name="dft_stage1",
    )(x, m)


def _spectral_kernel(y_ref, *refs, n1):
    ks = y_ref.shape[1]
    yf_refs, (g_ref, gi_ref, o_ref) = refs[:ks], refs[ks:]
    for k in range(ks):
        g = g_ref[k]
        z = jnp.dot(g, y_ref[:, k].reshape(2 * FFT_N2, -1), preferred_element_type=F32)
        k1 = pl.program_id(0) * ks + k
        sign = jnp.where(2 * k1 > n1, -1.0, 1.0).astype(BF16)
        yf = jnp.concatenate([yf_refs[k][0], yf_refs[k][1] * sign], axis=0)
        kf = jnp.dot(g, yf, preferred_element_type=F32)
        zr, zi = z[:FFT_N2], z[FFT_N2:]
        kr, ki = kf[:FFT_N2], kf[FFT_N2:]
        w = jnp.concatenate([zr * kr - zi * ki, zr * ki + zi * kr], axis=0).astype(BF16)
        v = jnp.dot(gi_ref[k], w, preferred_element_type=F32)
        o_ref[:, k] = v.reshape(2, FFT_N2, -1).astype(o_ref.dtype)


def spectral_stage(y, yf, order, gblk, giblk):
    _, n1, n2, C = y.shape
    ks = K1_STEP if n1 % K1_STEP == 0 else 1
    blk = pl.BlockSpec((2, ks, n2, C), lambda k: (0, k, 0, 0))
    mat = pl.BlockSpec((ks, 2 * n2, 2 * n2), lambda k: (k, 0, 0))

    def filter_row(i):
        def index(k):
            k1 = k * ks + i
            return (order, 0, jnp.where(2 * k1 > n1, n1 - k1, k1), 0, 0)
        return pl.BlockSpec((None, 2, None, n2, C), index)

    return pl.pallas_call(
        functools.partial(_spectral_kernel, n1=n1),
        grid=(n1 // ks,),
        in_specs=[blk] + [filter_row(i) for i in range(ks)] + [mat, mat],
        out_specs=blk,
        out_shape=jax.ShapeDtypeStruct(y.shape, BF16),
        compiler_params=pltpu.CompilerParams(
            dimension_semantics=("parallel",), vmem_limit_bytes=VMEM_LIMIT),
        name="hyena_spectral",
    )(y, *([yf] * ks), gblk, giblk)


def _idft_gate_kernel(v_ref, m_ref, gate_ref, z_ref, skip_ref, o_ref, xs, ys):
    conv = _columns_matmul(v_ref, m_ref[...], xs, ys).reshape(o_ref.shape)
    z = z_ref[...].astype(F32)
    o_ref[...] = (gate_ref[...].astype(F32) * (conv + z * skip_ref[...])).astype(o_ref.dtype)


def idft_gate(v, m1inv, uc5, gate_block, z5, z_block, skip):
    _, n1, n2, C = v.shape
    B, r = uc5.shape[0], uc5.shape[1]
    ct = C // WIDE
    rows = lambda blk: pl.BlockSpec((B, r, COLS, WIDE), lambda j, c: (0, 0, j, blk * ct + c))
    return pl.pallas_call(
        _idft_gate_kernel,
        grid=(n2 // COLS, ct),
        in_specs=[
            pl.BlockSpec((2, n1, COLS, WIDE), lambda j, c: (0, 0, j, c)),
            pl.BlockSpec(m1inv.shape, lambda j, c: (0, 0)),
            rows(gate_block), rows(z_block),
            pl.BlockSpec((1, WIDE), lambda j, c: (0, c)),
        ],
        out_specs=rows(0),
        out_shape=jax.ShapeDtypeStruct((B, r, n2, C), BF16),
        scratch_shapes=[pltpu.VMEM((SLABS, 2 * n1 * SUB, LANES), F32),
                        pltpu.VMEM((SLABS, B * r * SUB, LANES), F32)],
        compiler_params=pltpu.CompilerParams(
            dimension_semantics=("parallel", "parallel"), vmem_limit_bytes=VMEM_LIMIT),
        name="hyena_idft_gate",
    )(v, m1inv, uc5, z5, skip.reshape(1, C))


def hyena_mixer(p, conv_w, conv_b, fw1, fb1, fw2, fb2, fw3, fb3, freq, wout, skip):
    B, T, _ = p.shape
    assert B == 2, "the two batch rows are packed as one complex sequence"
    r, n1 = _fft_dims(T)
    n2 = FFT_N2
    Tp = r * n2
    m1, m1f, m1inv, gblk, giblk = _fft_tables(T)
    kc = hyena_filter(T, fw1, fb1, fw2, fb2, fw3, fb3, freq, wout)
    yf = dft_stage1(kc.reshape(HY_ORDER, 1, n1, n2, HY_WIDTH), m1f, 0)
    uc = hyena_short_conv(p, conv_w, conv_b, Tp)
    uc4 = uc.reshape(B, r, n2, (HY_ORDER + 1) * HY_WIDTH)
    z4, zblk = uc4, HY_ORDER
    for o in range(HY_ORDER):
        y = dft_stage1(z4[None], m1, zblk)[0]
        v = spectral_stage(y, yf, o, gblk, giblk)
        z4, zblk = idft_gate(v, m1inv, uc4, o, z4, zblk, skip[o]), 0
    return z4.reshape(B, Tp, HY_WIDTH)


def kernel(x, meta_tokens, norm1_g, w_in, hy_conv_w, hy_conv_b, hy_f_w1, hy_f_b1, hy_f_w2, hy_f_b2, hy_f_w3, hy_f_b3, hy_f_freq, hy_f_wout, hy_skip, lru_conv_w, lru_conv_b, lru_gate_w, lru_gate_b, lru_lambda, w_branch, w_out, norm2_g, ffn_w_gu, ffn_w_down, final_g):
    B = x.shape[0]
    meta = jnp.broadcast_to(meta_tokens[None].astype(x.dtype), (B, N_META, D_MODEL))
    h = jnp.concatenate([meta, x], axis=1)
    T = h.shape[1]
    w_in, w_branch, w_out, ffn_w_gu, ffn_w_down = (
        w.astype(BF16) for w in (w_in, w_branch, w_out, ffn_w_gu, ffn_w_down))
    for l in range(DEPTH):
        p = norm_proj(h.reshape(B * T, D_MODEL), norm1_g[l], w_in, l).reshape(B, T, D_IN)
        y_hy = hyena_mixer(p, hy_conv_w[l], hy_conv_b[l], hy_f_w1[l], hy_f_b1[l], hy_f_w2[l], hy_f_b2[l],
                           hy_f_w3[l], hy_f_b3[l], hy_f_freq[l], hy_f_wout[l], hy_skip[l])
        y_lru = rglru_mixer(p, lru_conv_w[l], lru_conv_b[l], lru_gate_w[l], lru_gate_b[l], lru_lambda[l])
        y_ret = retention_mixer(p)
        h = merge_out(h, y_hy, y_lru, y_ret, p, w_branch, w_out, l)
        if l == DEPTH - 1:
            return ffn_final(h, norm2_g[l], ffn_w_gu, ffn_w_down, final_g, l)
        h = ffn(h.reshape(B * T, D_MODEL), norm2_g[l], ffn_w_gu, ffn_w_down,
                final_g, l, final_norm=False).reshape(B, T, D_MODEL)
```
